```python
import jax, jax.numpy as jnp
from jax import lax
import numpy as np

D_MODEL = 1024
BATCH = 4
SEQ = 4096
DEPTH = 1

D_PLE = 256
ATTN_HEADS = 8
ATTN_HEAD_DIM = 64
ATTN_WIDTH = ATTN_HEADS * ATTN_HEAD_DIM
RET_HEADS = 4
RET_HEAD_DIM = 128
RET_WIDTH = RET_HEADS * RET_HEAD_DIM
MIX_WIDTH = ATTN_WIDTH + RET_WIDTH
IN_WIDTH = 4 * ATTN_WIDTH + 4 * RET_WIDTH
MOBA_BLOCK = 256
MOBA_TOPK = 3
MOBA_Q_CHUNK = 64
RET_CHUNK = 128
ROPE_BASE = 10000.0
DEEPNORM_ALPHA = (2.0 * DEPTH) ** 0.25
DEEPNORM_BETA = (8.0 * DEPTH) ** -0.25
LN_EPS = 1e-5
GN_EPS = 1e-6

kernel_name = "moba_retention_hymba_deepnorm_layer"

F32 = jnp.float32


def moba_attention(q, k, v):
    b, s, h, dh = q.shape
    nb = -(-s // MOBA_BLOCK)
    s_pad = nb * MOBA_BLOCK
    topk = min(MOBA_TOPK, nb)
    scale = dh ** -0.5
    qt = jnp.transpose(q, (0, 2, 1, 3))
    pad = ((0, 0), (0, 0), (0, s_pad - s), (0, 0))
    k_blk = jnp.pad(jnp.transpose(k, (0, 2, 1, 3)), pad).reshape(b, h, nb, MOBA_BLOCK, dh)
    v_blk = jnp.pad(jnp.transpose(v, (0, 2, 1, 3)), pad).reshape(b, h, nb, MOBA_BLOCK, dh)
    k_mean = jnp.mean(k_blk.astype(F32), axis=3)
    nq = s // MOBA_Q_CHUNK
    q_chunks = jnp.moveaxis(qt.reshape(b, h, nq, MOBA_Q_CHUNK, dh), 2, 0)
    bi = jnp.arange(b)[:, None, None, None]
    hi = jnp.arange(h)[None, :, None, None]
    blk_ids = jnp.arange(nb)
    rank = jnp.arange(topk)
    qi = jnp.arange(MOBA_Q_CHUNK)
    kj = jnp.arange(MOBA_BLOCK)

    def chunk(args):
        qc, ci = args
        q_start = ci * MOBA_Q_CHUNK
        own = q_start // MOBA_BLOCK
        gate = jnp.einsum('bhqd,bhnd->bhqn', qc.astype(F32), k_mean)
        gate = jnp.where(blk_ids < own, gate, -jnp.inf)
        _, sel = lax.top_k(gate, topk)
        valid = rank < own
        k_sel = k_blk[bi, hi, sel]
        v_sel = v_blk[bi, hi, sel]
        s_sel = jnp.einsum('bhqd,bhqkjd->bhqkj', qc, k_sel, preferred_element_type=F32) * scale
        s_sel = jnp.where(valid[:, None], s_sel, -jnp.inf)
        k_own = lax.dynamic_index_in_dim(k_blk, own, axis=2, keepdims=False)
        v_own = lax.dynamic_index_in_dim(v_blk, own, axis=2, keepdims=False)
        s_own = jnp.einsum('bhqd,bhjd->bhqj', qc, k_own, preferred_element_type=F32) * scale
        causal = (own * MOBA_BLOCK + kj)[None, :] <= (q_start + qi)[:, None]
        s_own = jnp.where(causal, s_own, -jnp.inf)
        n_sel = topk * MOBA_BLOCK
        scores = jnp.concatenate([s_sel.reshape(b, h, MOBA_Q_CHUNK, n_sel), s_own], axis=-1)
        probs = jax.nn.softmax(scores, axis=-1)
        p_sel = probs[..., :n_sel].reshape(b, h, MOBA_Q_CHUNK, topk, MOBA_BLOCK).astype(v_sel.dtype)
        p_own = probs[..., n_sel:].astype(v_own.dtype)
        out = (jnp.einsum('bhqkj,bhqkjd->bhqd', p_sel, v_sel, preferred_element_type=F32)
               + jnp.einsum('bhqj,bhjd->bhqd', p_own, v_own, preferred_element_type=F32))
        return out.astype(qc.dtype)

    out = lax.map(chunk, (q_chunks, jnp.arange(nq)))
    out = jnp.moveaxis(out, 0, 2).reshape(b, h, s, dh)
    return jnp.transpose(out, (0, 2, 1, 3)).reshape(b, s, h * dh)


def rotary(x, pos):
    half = x.shape[-1] // 2
    freqs = ROPE_BASE ** (-jnp.arange(half, dtype=F32) / half)
    ang = pos.astype(F32)[:, None] * freqs[None, :]
    cos, sin = jnp.cos(ang), jnp.sin(ang)
    x1, x2 = x[..., :half], x[..., half:]
    return jnp.concatenate([x1 * cos - x2 * sin, x1 * sin + x2 * cos], axis=-1)


def retention(q, k, v):
    b, s, h, d = q.shape
    pos = jnp.arange(s)
    qt = rotary(jnp.transpose(q, (0, 2, 1, 3)).astype(F32), pos)
    kt = rotary(jnp.transpose(k, (0, 2, 1, 3)).astype(F32), pos) * (d ** -0.5)
    vt = jnp.transpose(v, (0, 2, 1, 3)).astype(F32)
    log_g = jnp.log1p(-jnp.exp2(-5.0 - jnp.arange(h, dtype=F32)))
    c = RET_CHUNK
    nc = s // c
    idx = jnp.arange(c, dtype=F32)
    rel = idx[:, None] - idx[None, :]
    decay = jnp.where(rel >= 0, jnp.exp(jnp.maximum(rel, 0.0)[None] * log_g[:, None, None]), 0.0)
    xi = jnp.exp((idx + 1.0)[None] * log_g[:, None])
    zeta = jnp.exp((c - 1.0 - idx)[None] * log_g[:, None])
    g_chunk = jnp.exp(c * log_g)

    def to_chunks(t):
        return jnp.moveaxis(t.reshape(b, h, nc, c, d), 2, 0)

    def step(state, qkv):
        qc, kc, vc = qkv
        inner = jnp.einsum('bhqd,bhkd->bhqk', qc, kc) * decay
        out = (jnp.einsum('bhqk,bhkd->bhqd', inner, vc)
               + jnp.einsum('bhqd,bhde->bhqe', qc, state) * xi[None, :, :, None])
        state = (state * g_chunk[None, :, None, None]
                 + jnp.einsum('bhkd,bhke->bhde', kc * zeta[None, :, :, None], vc))
        return state, out

    state0 = jnp.zeros((b, h, d, d), F32)
    _, out = lax.scan(step, state0, (to_chunks(qt), to_chunks(kt), to_chunks(vt)))
    out = jnp.moveaxis(out, 0, 2).reshape(b, h, s, d)
    return jnp.transpose(out, (0, 2, 1, 3))


def head_group_norm(o, gain):
    mu = jnp.mean(o, axis=-1, keepdims=True)
    var = jnp.mean(jnp.square(o - mu), axis=-1, keepdims=True)
    on = (o - mu) * lax.rsqrt(var + GN_EPS)
    b, s, h, d = o.shape
    return on.reshape(b, s, h * d) * gain.astype(F32)


def layer_norm(u, gain, bias):
    uf = u.astype(F32)
    mu = jnp.mean(uf, axis=-1, keepdims=True)
    var = jnp.mean(jnp.square(uf - mu), axis=-1, keepdims=True)
    y = (uf - mu) * lax.rsqrt(var + LN_EPS) * gain.astype(F32) + bias.astype(F32)
    return y.astype(u.dtype)


def setup_inputs(seed: int = 0) -> dict:
    key = jax.random.key(seed)
    ks = jax.random.split(key, 9)
    x = jax.random.normal(ks[0], (BATCH, SEQ, D_MODEL), F32)
    p = jax.random.normal(ks[1], (DEPTH, BATCH, SEQ, D_PLE), F32)
    col_scale = np.ones((IN_WIDTH,), np.float32)
    col_scale[2 * ATTN_WIDTH:3 * ATTN_WIDTH] = DEEPNORM_BETA
    r0 = 4 * ATTN_WIDTH
    col_scale[r0 + 2 * RET_WIDTH:r0 + 3 * RET_WIDTH] = DEEPNORM_BETA
    w_in = (jax.random.normal(ks[2], (DEPTH, D_MODEL, IN_WIDTH), F32)
            * (D_MODEL ** -0.5) * jnp.asarray(col_scale))
    w_out = jax.random.normal(ks[3], (DEPTH, MIX_WIDTH, D_MODEL), F32) * (MIX_WIDTH ** -0.5) * DEEPNORM_BETA
    ret_gn_gain = 1.0 + 0.02 * jax.random.normal(ks[4], (DEPTH, RET_WIDTH), F32)
    w_ple_gate = jax.random.normal(ks[5], (DEPTH, D_MODEL, D_MODEL), F32) * (D_MODEL ** -0.5)
    w_ple_proj = jax.random.normal(ks[6], (DEPTH, D_PLE, D_MODEL), F32) * (D_PLE ** -0.5) * DEEPNORM_BETA
    ln_gain = 1.0 + 0.02 * jax.random.normal(ks[7], (DEPTH, D_MODEL), F32)
    ln_bias = 0.02 * jax.random.normal(ks[8], (DEPTH, D_MODEL), F32)
    return {"x": x, "p": p, "w_in": w_in, "w_out": w_out, "ret_gn_gain": ret_gn_gain,
            "w_ple_gate": w_ple_gate, "w_ple_proj": w_ple_proj,
            "ln_gain": ln_gain, "ln_bias": ln_bias}


def reference(x, p, w_in, w_out, ret_gn_gain, w_ple_gate, w_ple_proj, ln_gain, ln_bias):
    b, s, _ = x.shape
    h = x
    for i in range(DEPTH):
        z = h @ w_in[i]
        a = ATTN_WIDTH
        r = RET_WIDTH
        o = 4 * a
        aq, ak, av, ag = z[..., 0:a], z[..., a:2 * a], z[..., 2 * a:3 * a], z[..., 3 * a:o]
        rq, rk = z[..., o:o + r], z[..., o + r:o + 2 * r]
        rv, rg = z[..., o + 2 * r:o + 3 * r], z[..., o + 3 * r:o + 4 * r]
        ahs = (b, s, ATTN_HEADS, ATTN_HEAD_DIM)
        rhs = (b, s, RET_HEADS, RET_HEAD_DIM)
        attn = moba_attention(aq.reshape(ahs), ak.reshape(ahs), av.reshape(ahs)) * jax.nn.silu(ag)
        ret = retention(rq.reshape(rhs), rk.reshape(rhs), rv.reshape(rhs))
        ret = head_group_norm(ret, ret_gn_gain[i]).astype(h.dtype) * jax.nn.silu(rg)
        mix = jnp.concatenate([attn, ret], axis=-1) @ w_out[i]
        u = DEEPNORM_ALPHA * h + mix
        ple = p[i] @ w_ple_proj[i]
        u = u + jax.nn.sigmoid(u @ w_ple_gate[i]) * ple
        h = layer_norm(u, ln_gain[i], ln_bias[i])
    return h
```

```python
import functools

import jax
import jax.numpy as jnp
from jax import lax
from jax.experimental import pallas as pl
from jax.experimental.pallas import tpu as pltpu

F32 = jnp.float32
BF16 = jnp.bfloat16

D_MODEL = 1024
D_PLE = 256
ATTN_HEADS = 8
ATTN_HEAD_DIM = 64
ATTN_WIDTH = ATTN_HEADS * ATTN_HEAD_DIM
RET_HEADS = 4
RET_HEAD_DIM = 128
RET_WIDTH = RET_HEADS * RET_HEAD_DIM
IN_WIDTH = 4 * ATTN_WIDTH + 4 * RET_WIDTH
MOBA_BLOCK = 256
MOBA_TOPK = 3
ROPE_BASE = 10000.0
DEPTH = 1
DEEPNORM_ALPHA = (2.0 * DEPTH) ** 0.25
LN_EPS = 1e-5
GN_EPS = 1e-6

LANES = 128
HEADS_PER_STEP = LANES // ATTN_HEAD_DIM
MASKED = -1e30
ONES_ROWS = 16
RET_CHUNK = 256
VMEM_LIMIT = 48 * 1024 * 1024


def _proj_kernel(x_ref, w_ref, o_ref, *, n_chunk):
    xb = x_ref[...].astype(BF16)
    for j in range(o_ref.shape[1] // n_chunk):
        cols = slice(j * n_chunk, (j + 1) * n_chunk)
        o_ref[:, cols] = jnp.dot(xb, w_ref[:, cols], preferred_element_type=F32).astype(o_ref.dtype)


def _input_projection(x2d, w_in_bf16, *, tm=512, n_chunk=1024):
    m, k = x2d.shape
    n = w_in_bf16.shape[1]
    return pl.pallas_call(
        functools.partial(_proj_kernel, n_chunk=n_chunk),
        grid=(m // tm,),
        in_specs=[pl.BlockSpec((tm, k), lambda i: (i, 0)),
                  pl.BlockSpec((k, n), lambda i: (0, 0))],
        out_specs=pl.BlockSpec((tm, n), lambda i: (i, 0)),
        out_shape=jax.ShapeDtypeStruct((m, n), BF16),
        compiler_params=pltpu.CompilerParams(dimension_semantics=("arbitrary",),
                                             vmem_limit_bytes=VMEM_LIMIT),
        name="input_projection",
    )(x2d, w_in_bf16)


def _moba_kernel(q_ref, k_ref, v_ref, g_ref, o_ref, vt_ref, kmean_ref, bias_ref, acc_ref, *, nb):
    qi = pl.program_id(2)
    blk = MOBA_BLOCK
    hd = ATTN_HEAD_DIM

    @pl.when(qi == 0)
    def _prepare_keys_and_values():
        lane = lax.broadcasted_iota(jnp.int32, (nb, LANES), 1)
        for n in range(nb):
            rows = slice(n * blk, (n + 1) * blk)
            vt_ref[n, 0:LANES, :] = v_ref[0, rows, :].astype(F32).T.astype(BF16)
            vt_ref[n, LANES:LANES + ONES_ROWS, :] = jnp.ones((ONES_ROWS, blk), BF16)
        key_blk = lax.broadcasted_iota(jnp.int32, (nb, nb * blk), 1) // blk
        row_blk = lax.broadcasted_iota(jnp.int32, (nb, nb * blk), 0)
        averager = jnp.where(key_blk == row_blk, 1.0 / blk, 0.0).astype(BF16)
        kmean = jnp.dot(averager, k_ref[0], preferred_element_type=F32)
        hi = kmean.astype(BF16)
        lo = (kmean - hi.astype(F32)).astype(BF16)
        zero = jnp.zeros_like(hi)
        for h in range(HEADS_PER_STEP):
            in_head = (lane >= h * hd) & (lane < (h + 1) * hd)
            kmean_ref[(2 * h) * nb:(2 * h + 1) * nb, :] = jnp.where(in_head, hi, zero)
            kmean_ref[(2 * h + 1) * nb:(2 * h + 2) * nb, :] = jnp.where(in_head, lo, zero)

    q = q_ref[0].astype(F32) * (hd ** -0.5)
    qt = q.T.astype(BF16)
    feat = lax.broadcasted_iota(jnp.int32, (LANES, blk), 0)
    qt_heads = [jnp.where((feat >= h * hd) & (feat < (h + 1) * hd), qt, jnp.zeros_like(qt))
                for h in range(HEADS_PER_STEP)]

    gates = jnp.dot(kmean_ref[...], qt, preferred_element_type=F32)
    blk_id = lax.broadcasted_iota(jnp.int32, (nb, blk), 0)
    past = blk_id < qi
    for h in range(HEADS_PER_STEP):
        g = gates[(2 * h) * nb:(2 * h + 1) * nb] + gates[(2 * h + 1) * nb:(2 * h + 2) * nb]
        g = jnp.where(past, g, -jnp.inf)
        rank = jnp.zeros((nb, blk), F32)
        for m in range(nb):
            gm = g[m:m + 1, :]
            tie_first = (blk_id > m).astype(F32)
            rank = rank + jnp.where(gm > g, 1.0, jnp.where(gm == g, tie_first, 0.0))
        selected = (rank < MOBA_TOPK) & past
        bias_ref[h] = jnp.where(selected, 0.0, MASKED)

    key_pos = lax.broadcasted_iota(jnp.int32, (blk, blk), 0)
    q_pos = lax.broadcasted_iota(jnp.int32, (blk, blk), 1)
    own_rows = pl.ds(pl.multiple_of(qi * blk, blk), blk)
    k_own = k_ref[0, own_rows, :]
    vt_own = vt_ref[qi]
    m_init = []
    for h in range(HEADS_PER_STEP):
        s = jnp.dot(k_own, qt_heads[h], preferred_element_type=F32)
        s = jnp.where(key_pos <= q_pos, s, MASKED)
        m = jnp.max(s, axis=0, keepdims=True)
        p = jnp.exp(s - m).astype(BF16)
        acc_ref[h] = jnp.dot(vt_own, p, preferred_element_type=F32)
        m_init.append(m)

    def past_block(n, m_run):
        rows = pl.ds(pl.multiple_of(n * blk, blk), blk)
        k_blk = k_ref[0, rows, :]
        vt_blk = vt_ref[n]
        m_out = []
        for h in range(HEADS_PER_STEP):
            s = jnp.dot(k_blk, qt_heads[h], preferred_element_type=F32)
            s = s + bias_ref[h, pl.ds(n, 1), :]
            m_new = jnp.maximum(m_run[h], jnp.max(s, axis=0, keepdims=True))
            alpha = jnp.exp(m_run[h] - m_new)
            p = jnp.exp(s - m_new).astype(BF16)
            acc_ref[h] = alpha * acc_ref[h] + jnp.dot(vt_blk, p, preferred_element_type=F32)
            m_out.append(m_new)
        return tuple(m_out)

    lax.fori_loop(0, qi, past_block, tuple(m_init))

    outs = []
    for h in range(HEADS_PER_STEP):
        denom = acc_ref[h, LANES:LANES + 1, :]
        outs.append(acc_ref[h, h * hd:(h + 1) * hd, :] / denom)
    attn = jnp.concatenate(outs, axis=0).T
    gate = g_ref[0].astype(F32)
    o_ref[0] = (attn * (gate * jax.nn.sigmoid(gate))).astype(o_ref.dtype)


def _moba_attention(z3d):
    b, s, _ = z3d.shape
    nb = s // MOBA_BLOCK
    steps = ATTN_WIDTH // LANES
    kernel = functools.partial(_moba_kernel, nb=nb)
    return pl.pallas_call(
        kernel,
        grid=(b, steps, nb),
        in_specs=[pl.BlockSpec((1, MOBA_BLOCK, LANES), lambda bi, hp, qi: (bi, qi, hp)),
                  pl.BlockSpec((1, s, LANES), lambda bi, hp, qi: (bi, 0, steps + hp)),
                  pl.BlockSpec((1, s, LANES), lambda bi, hp, qi: (bi, 0, 2 * steps + hp)),
                  pl.BlockSpec((1, MOBA_BLOCK, LANES), lambda bi, hp, qi: (bi, qi, 3 * steps + hp))],
        out_specs=pl.BlockSpec((1, MOBA_BLOCK, LANES), lambda bi, hp, qi: (bi, qi, hp)),
        out_shape=jax.ShapeDtypeStruct((b, s, ATTN_WIDTH), BF16),
        scratch_shapes=[pltpu.VMEM((nb, LANES + ONES_ROWS, MOBA_BLOCK), BF16),
                        pltpu.VMEM((2 * HEADS_PER_STEP * nb, LANES), BF16),
                        pltpu.VMEM((HEADS_PER_STEP, nb, MOBA_BLOCK), F32),
                        pltpu.VMEM((HEADS_PER_STEP, LANES + ONES_ROWS, MOBA_BLOCK), F32)],
        compiler_params=pltpu.CompilerParams(dimension_semantics=("arbitrary",) * 3,
                                             vmem_limit_bytes=VMEM_LIMIT),
        name="moba_attention",
    )(z3d, z3d, z3d, z3d)


def _retention_kernel(q_ref, k_ref, v_ref, g_ref, cos_ref, sin_ref, decay_ref, xi_ref, zeta_ref,
                      gc_ref, gain_ref, o_ref, state_ref):
    c = pl.program_id(2)
    d = RET_HEAD_DIM

    @pl.when(c == 0)
    def _reset_state():
        state_ref[...] = jnp.zeros_like(state_ref)

    cos = cos_ref[...]
    sin = sin_ref[...]

    def rotate(t):
        return t * cos + pltpu.roll(t, d // 2, 1) * sin

    qr = rotate(q_ref[0].astype(F32))
    kr = rotate(k_ref[0].astype(F32)) * (d ** -0.5)
    qb = qr.astype(BF16)
    vb = v_ref[0]
    scores = lax.dot_general(qb, kr.astype(BF16), (((1,), (1,)), ((), ())),
                             preferred_element_type=F32)
    inner = (scores * decay_ref[0]).astype(BF16)
    state = state_ref[...]
    out = (jnp.dot(inner, vb, preferred_element_type=F32)
           + jnp.dot(qb, state.astype(BF16), preferred_element_type=F32) * xi_ref[0])
    kz_t = (kr * zeta_ref[0]).T.astype(BF16)
    state_ref[...] = state * gc_ref[0] + jnp.dot(kz_t, vb, preferred_element_type=F32)

    mu = jnp.mean(out, axis=-1, keepdims=True)
    cen = out - mu
    var = jnp.mean(cen * cen, axis=-1, keepdims=True)
    normed = cen * lax.rsqrt(var + GN_EPS) * gain_ref[0]
    gate = g_ref[0].astype(F32)
    o_ref[0] = (normed * (gate * jax.nn.sigmoid(gate))).astype(o_ref.dtype)


def _retention_constants(s):
    d, c = RET_HEAD_DIM, RET_CHUNK
    half = d // 2
    freqs = ROPE_BASE ** (-jnp.arange(half, dtype=F32) / half)
    ang = jnp.arange(s).astype(F32)[:, None] * freqs[None, :]
    cos, sin = jnp.cos(ang), jnp.sin(ang)
    cos_full = jnp.concatenate([cos, cos], axis=-1)
    sin_signed = jnp.concatenate([-sin, sin], axis=-1)
    log_g = jnp.log1p(-jnp.exp2(-5.0 - jnp.arange(RET_HEADS, dtype=F32)))
    idx = jnp.arange(c, dtype=F32)
    rel = idx[:, None] - idx[None, :]
    decay = jnp.where(rel >= 0, jnp.exp(jnp.maximum(rel, 0.0)[None] * log_g[:, None, None]), 0.0)
    xi = jnp.exp((idx + 1.0)[None] * log_g[:, None])
    zeta = jnp.exp((c - 1.0 - idx)[None] * log_g[:, None])
    g_chunk = jnp.exp(c * log_g)
    bcast = lambda t: jnp.broadcast_to(t[..., None], t.shape + (d,))
    return cos_full, sin_signed, decay, bcast(xi), bcast(zeta), bcast(g_chunk[:, None])


def _retention(z3d, ret_gn_gain):
    b, s, _ = z3d.shape
    c, d = RET_CHUNK, RET_HEAD_DIM
    base = 4 * ATTN_WIDTH // LANES
    cos, sin, decay, xi, zeta, gc = _retention_constants(s)
    gain = ret_gn_gain.astype(F32).reshape(RET_HEADS, 1, d)
    zspec = lambda off: pl.BlockSpec((1, c, d), lambda bi, h, ci: (bi, ci, base + off * RET_HEADS + h))
    per_head = lambda rows: pl.BlockSpec((1, rows, d), lambda bi, h, ci: (h, 0, 0))
    return pl.pallas_call(
        _retention_kernel,
        grid=(b, RET_HEADS, s // c),
        in_specs=[zspec(0), zspec(1), zspec(2), zspec(3),
                  pl.BlockSpec((c, d), lambda bi, h, ci: (ci, 0)),
                  pl.BlockSpec((c, d), lambda bi, h, ci: (ci, 0)),
                  pl.BlockSpec((1, c, c), lambda bi, h, ci: (h, 0, 0)),
                  per_head(c), per_head(c), per_head(1), per_head(1)],
        out_specs=pl.BlockSpec((1, c, d), lambda bi, h, ci: (bi, ci, h)),
        out_shape=jax.ShapeDtypeStruct((b, s, RET_WIDTH), BF16),
        scratch_shapes=[pltpu.VMEM((d, d), F32)],
        compiler_params=pltpu.CompilerParams(dimension_semantics=("arbitrary",) * 3,
                                             vmem_limit_bytes=VMEM_LIMIT),
        name="retention",
    )(z3d, z3d, z3d, z3d, cos, sin, decay, xi, zeta, gc, gain)


def _output_kernel(x_ref, a_ref, r_ref, p_ref, wa_ref, wr_ref, wg_ref, wp_ref, gain_ref, bias_ref, o_ref):
    mix = (jnp.dot(a_ref[...], wa_ref[...], preferred_element_type=F32)
           + jnp.dot(r_ref[...], wr_ref[...], preferred_element_type=F32))
    u = DEEPNORM_ALPHA * x_ref[...] + mix
    ple = jnp.dot(p_ref[...].astype(BF16), wp_ref[...], preferred_element_type=F32)
    gate = jax.nn.sigmoid(jnp.dot(u.astype(BF16), wg_ref[...], preferred_element_type=F32))
    u = u + gate * ple
    mu = jnp.mean(u, axis=-1, keepdims=True)
    cen = u - mu
    var = jnp.mean(cen * cen, axis=-1, keepdims=True)
    o_ref[...] = cen * lax.rsqrt(var + LN_EPS) * gain_ref[...] + bias_ref[...]


def _output_stage(x2d, attn2d, ret2d, p2d, w_out_bf16, w_gate_bf16, w_proj_bf16, ln_gain, ln_bias, *, tm=512):
    m, dm = x2d.shape
    rows = lambda width: pl.BlockSpec((tm, width), lambda i: (i, 0))
    whole = lambda shape: pl.BlockSpec(shape, lambda i: (0, 0))
    return pl.pallas_call(
        _output_kernel,
        grid=(m // tm,),
        in_specs=[rows(dm), rows(ATTN_WIDTH), rows(RET_WIDTH), rows(D_PLE),
                  pl.BlockSpec((ATTN_WIDTH, dm), lambda i: (0, 0)),
                  pl.BlockSpec((RET_WIDTH, dm), lambda i: (1, 0)),
                  whole((dm, dm)), whole((D_PLE, dm)), whole((1, dm)), whole((1, dm))],
        out_specs=rows(dm),
        out_shape=jax.ShapeDtypeStruct((m, dm), F32),
        compiler_params=pltpu.CompilerParams(dimension_semantics=("arbitrary",),
                                             vmem_limit_bytes=VMEM_LIMIT),
        name="output_stage",
    )(x2d, attn2d, ret2d, p2d, w_out_bf16, w_out_bf16, w_gate_bf16, w_proj_bf16,
      ln_gain.astype(F32).reshape(1, dm), ln_bias.astype(F32).reshape(1, dm))


def kernel(x, p, w_in, w_out, ret_gn_gain, w_ple_gate, w_ple_proj, ln_gain, ln_bias):
    b, s, dm = x.shape
    h = x
    for i in range(DEPTH):
        h2d = h.reshape(b * s, dm)
        z = _input_projection(h2d, w_in[i].astype(BF16))
        z3d = z.reshape(b, s, IN_WIDTH)
        attn = _moba_attention(z3d)
        ret = _retention(z3d, ret_gn_gain[i])
        out = _output_stage(h2d, attn.reshape(b * s, ATTN_WIDTH), ret.reshape(b * s, RET_WIDTH),
                            p[i].reshape(b * s, D_PLE), w_out[i].astype(BF16), w_ple_gate[i].astype(BF16),
                            w_ple_proj[i].astype(BF16), ln_gain[i], ln_bias[i])
        h = out.reshape(b, s, dm)
    return h
```

```python
import functools

import jax
import jax.numpy as jnp
from jax import lax
from jax.experimental import pallas as pl
from jax.experimental.pallas import tpu as pltpu

F32 = jnp.float32
BF16 = jnp.bfloat16

D_MODEL = 1024
D_PLE = 256
ATTN_HEADS = 8
ATTN_HEAD_DIM = 64
ATTN_WIDTH = ATTN_HEADS * ATTN_HEAD_DIM
RET_HEADS = 4
RET_HEAD_DIM = 128
RET_WIDTH = RET_HEADS * RET_HEAD_DIM
IN_WIDTH = 4 * ATTN_WIDTH + 4 * RET_WIDTH
MOBA_BLOCK = 256
MOBA_TOPK = 3
ROPE_BASE = 10000.0
DEPTH = 1
DEEPNORM_ALPHA = (2.0 * DEPTH) ** 0.25
LN_EPS = 1e-5
GN_EPS = 1e-6

LANES = 128
HEADS_PER_STEP = LANES // ATTN_HEAD_DIM
MASKED = -1e30
ONES_ROWS = 16
CHUNK_BLOCKS = 4
RET_CHUNK = 256
VMEM_LIMIT = 48 * 1024 * 1024


def _proj_kernel(x_ref, w_ref, o_ref, *, n_chunk):
    xb = x_ref[...].astype(BF16)
    for j in range(o_ref.shape[1] // n_chunk):
        cols = slice(j * n_chunk, (j + 1) * n_chunk)
        o_ref[:, cols] = jnp.dot(xb, w_ref[:, cols], preferred_element_type=F32).astype(o_ref.dtype)


def _input_projection(x2d, w_in_bf16, *, tm=512, n_chunk=1024):
    m, k = x2d.shape
    n = w_in_bf16.shape[1]
    return pl.pallas_call(
        functools.partial(_proj_kernel, n_chunk=n_chunk),
        grid=(m // tm,),
        in_specs=[pl.BlockSpec((tm, k), lambda i: (i, 0)),
                  pl.BlockSpec((k, n), lambda i: (0, 0))],
        out_specs=pl.BlockSpec((tm, n), lambda i: (i, 0)),
        out_shape=jax.ShapeDtypeStruct((m, n), BF16),
        compiler_params=pltpu.CompilerParams(dimension_semantics=("arbitrary",),
                                             vmem_limit_bytes=VMEM_LIMIT),
        name="input_projection",
    )(x2d, w_in_bf16)


def _moba_kernel(q_ref, k_ref, v_ref, g_ref, o_ref,
                 vtb_ref, vtc_ref, qtb_ref, bias_ref, s_ref, m_ref, acc_ref, *, nb):
    qi = pl.program_id(2)
    blk = MOBA_BLOCK
    hd = ATTN_HEAD_DIM
    nh = HEADS_PER_STEP
    width = nh * blk
    vrows = LANES + ONES_ROWS
    seq = nb * blk

    @pl.when(qi == 0)
    def _prepare_head_pair():
        ones = jnp.ones((ONES_ROWS, blk), BF16)
        for n in range(nb):
            vt = v_ref[0, n * blk:(n + 1) * blk, :].astype(F32).T.astype(BF16)
            c, j = divmod(n, CHUNK_BLOCKS)
            vtb_ref[n, 0:LANES, :] = vt
            vtb_ref[n, LANES:vrows, :] = ones
            vtc_ref[c, 0:LANES, j * blk:(j + 1) * blk] = vt
            vtc_ref[c, LANES:vrows, j * blk:(j + 1) * blk] = ones
        key_blk = lax.broadcasted_iota(jnp.int32, (nb, seq), 1) // blk
        row_blk = lax.broadcasted_iota(jnp.int32, (nb, seq), 0)
        averager = jnp.where(key_blk == row_blk, 1.0 / blk, 0.0).astype(BF16)
        kmean = jnp.dot(averager, k_ref[0], preferred_element_type=F32)
        hi = kmean.astype(BF16)
        lo = (kmean - hi.astype(F32)).astype(BF16)
        lane = lax.broadcasted_iota(jnp.int32, (nb, LANES), 1)
        parts = []
        for h in range(nh):
            in_head = (lane >= h * hd) & (lane < (h + 1) * hd)
            parts += [jnp.where(in_head, hi, jnp.zeros_like(hi)), jnp.where(in_head, lo, jnp.zeros_like(lo))]
        kstack = jnp.concatenate(parts, axis=0)
        feat = lax.broadcasted_iota(jnp.int32, (LANES, blk), 0)
        qts = []
        for n in range(nb):
            qt = (q_ref[0, n * blk:(n + 1) * blk, :].astype(F32) * (hd ** -0.5)).T.astype(BF16)
            for h in range(nh):
                in_head = (feat >= h * hd) & (feat < (h + 1) * hd)
                qtb_ref[n, :, h * blk:(h + 1) * blk] = jnp.where(in_head, qt, jnp.zeros_like(qt))
            qts.append(qt)
        gates = jnp.dot(kstack, jnp.concatenate(qts, axis=1), preferred_element_type=F32)
        past = row_blk < key_blk
        for h in range(nh):
            g = gates[(2 * h) * nb:(2 * h + 1) * nb] + gates[(2 * h + 1) * nb:(2 * h + 2) * nb]
            g = jnp.where(past, g, -jnp.inf)
            rank = jnp.zeros((nb, seq), F32)
            for m in range(nb):
                gm = g[m:m + 1, :]
                tie_first = (row_blk > m).astype(F32)
                rank = rank + jnp.where(gm > g, 1.0, jnp.where(gm == g, tie_first, 0.0))
            bias = jnp.where((rank < MOBA_TOPK) & past, 0.0, MASKED)
            for n in range(nb):
                bias_ref[n, :, h * blk:(h + 1) * blk] = bias[:, n * blk:(n + 1) * blk]

    qt_both = qtb_ref[qi]
    key_pos = lax.broadcasted_iota(jnp.int32, (blk, width), 0)
    q_pos = lax.broadcasted_iota(jnp.int32, (blk, width), 1) & (blk - 1)
    causal = key_pos <= q_pos

    def block_rows(n0, cnt):
        return pl.ds(pl.multiple_of(n0 * blk, blk), cnt * blk)

    def score_blocks(n0, cnt, own=False):
        s = jnp.dot(k_ref[0, block_rows(n0, cnt), :], qt_both, preferred_element_type=F32)
        m8 = None if own else m_ref[...]
        for j in range(cnt):
            sj = s[j * blk:(j + 1) * blk]
            if own:
                sj = jnp.where(causal, sj, MASKED)
            else:
                sj = sj + bias_ref[qi, pl.ds(n0 + j, 1), :]
            s_ref[block_rows(n0 + j, 1), :] = sj
            mj = jnp.max(sj.reshape(blk // 8, 8, width), axis=0)
            m8 = mj if m8 is None else jnp.maximum(m8, mj)
        m_ref[...] = m8

    def value_blocks(n0, cnt, vt, m):
        p = jnp.exp(s_ref[block_rows(n0, cnt), :] - m).astype(BF16)
        return jnp.dot(vt, p, preferred_element_type=F32)

    n_chunks = qi // CHUNK_BLOCKS
    rem = qi - n_chunks * CHUNK_BLOCKS
    rem_start = n_chunks * CHUNK_BLOCKS

    score_blocks(qi, 1, own=True)

    def score_chunk(c, carry):
        score_blocks(c * CHUNK_BLOCKS, CHUNK_BLOCKS)
        return carry

    lax.fori_loop(0, n_chunks, score_chunk, 0)
    for r in range(1, CHUNK_BLOCKS):
        @pl.when(rem == r)
        def _score_remainder(r=r):
            score_blocks(rem_start, r)

    m = jnp.max(m_ref[...], axis=0, keepdims=True)
    acc_ref[...] = value_blocks(qi, 1, vtb_ref[qi], m)

    def value_chunk(c, carry):
        acc_ref[...] += value_blocks(c * CHUNK_BLOCKS, CHUNK_BLOCKS, vtc_ref[c], m)
        return carry

    lax.fori_loop(0, n_chunks, value_chunk, 0)
    for r in range(1, CHUNK_BLOCKS):
        @pl.when(rem == r)
        def _value_remainder(r=r):
            acc_ref[...] += value_blocks(rem_start, r, vtc_ref[n_chunks, :, 0:r * blk], m)

    outs = []
    for h in range(nh):
        cols = slice(h * blk, (h + 1) * blk)
        outs.append(acc_ref[h * hd:(h + 1) * hd, cols] / acc_ref[LANES:LANES + 1, cols])
    attn = jnp.concatenate(outs, axis=0).T
    gate = g_ref[0].astype(F32)
    o_ref[0] = (attn * (gate * jax.nn.sigmoid(gate))).astype(o_ref.dtype)


def _moba_attention(z3d):
    b, s, _ = z3d.shape
    nb = s // MOBA_BLOCK
    steps = ATTN_WIDTH // LANES
    width = HEADS_PER_STEP * MOBA_BLOCK
    vrows = LANES + ONES_ROWS
    kernel = functools.partial(_moba_kernel, nb=nb)
    full = lambda off: pl.BlockSpec((1, s, LANES), lambda bi, hp, qi: (bi, 0, off * steps + hp))
    return pl.pallas_call(
        kernel,
        grid=(b, steps, nb),
        in_specs=[full(0), full(1), full(2),
                  pl.BlockSpec((1, MOBA_BLOCK, LANES), lambda bi, hp, qi: (bi, qi, 3 * steps + hp))],
        out_specs=pl.BlockSpec((1, MOBA_BLOCK, LANES), lambda bi, hp, qi: (bi, qi, hp)),
        out_shape=jax.ShapeDtypeStruct((b, s, ATTN_WIDTH), BF16),
        scratch_shapes=[pltpu.VMEM((nb, vrows, MOBA_BLOCK), BF16),
                        pltpu.VMEM((nb // CHUNK_BLOCKS, vrows, CHUNK_BLOCKS * MOBA_BLOCK), BF16),
                        pltpu.VMEM((nb, LANES, width), BF16),
                        pltpu.VMEM((nb, nb, width), F32),
                        pltpu.VMEM((s, width), F32),
                        pltpu.VMEM((8, width), F32),
                        pltpu.VMEM((vrows, width), F32)],
        compiler_params=pltpu.CompilerParams(dimension_semantics=("arbitrary",) * 3,
                                             vmem_limit_bytes=VMEM_LIMIT),
        name="moba_attention",
    )(z3d, z3d, z3d, z3d)


def _retention_kernel(q_ref, k_ref, v_ref, g_ref, cos_ref, sin_ref, decay_ref, xi_ref, zeta_ref,
                      gc_ref, gain_ref, o_ref, state_ref):
    c = pl.program_id(2)
    d = RET_HEAD_DIM

    @pl.when(c == 0)
    def _reset_state():
        state_ref[...] = jnp.zeros_like(state_ref)

    cos = cos_ref[...]
    sin = sin_ref[...]

    def rotate(t):
        return t * cos + pltpu.roll(t, d // 2, 1) * sin

    qr = rotate(q_ref[0].astype(F32))
    kr = rotate(k_ref[0].astype(F32)) * (d ** -0.5)
    qb = qr.astype(BF16)
    vb = v_ref[0]
    scores = lax.dot_general(qb, kr.astype(BF16), (((1,), (1,)), ((), ())),
                             preferred_element_type=F32)
    inner = (scores * decay_ref[0]).astype(BF16)
    state = state_ref[...]
    out = (jnp.dot(inner, vb, preferred_element_type=F32)
           + jnp.dot(qb, state.astype(BF16), preferred_element_type=F32) * xi_ref[0])
    kz_t = (kr * zeta_ref[0]).T.astype(BF16)
    state_ref[...] = state * gc_ref[0] + jnp.dot(kz_t, vb, preferred_element_type=F32)

    mu = jnp.mean(out, axis=-1, keepdims=True)
    cen = out - mu
    var = jnp.mean(cen * cen, axis=-1, keepdims=True)
    normed = cen * lax.rsqrt(var + GN_EPS) * gain_ref[0]
    gate = g_ref[0].astype(F32)
    o_ref[0] = (normed * (gate * jax.nn.sigmoid(gate))).astype(o_ref.dtype)


def _retention_constants(s):
    d, c = RET_HEAD_DIM, RET_CHUNK
    half = d // 2
    freqs = ROPE_BASE ** (-jnp.arange(half, dtype=F32) / half)
    ang = jnp.arange(s).astype(F32)[:, None] * freqs[None, :]
    cos, sin = jnp.cos(ang), jnp.sin(ang)
    cos_full = jnp.concatenate([cos, cos], axis=-1)
    sin_signed = jnp.concatenate([-sin, sin], axis=-1)
    log_g = jnp.log1p(-jnp.exp2(-5.0 - jnp.arange(RET_HEADS, dtype=F32)))
    idx = jnp.arange(c, dtype=F32)
    rel = idx[:, None] - idx[None, :]
    decay = jnp.where(rel >= 0, jnp.exp(jnp.maximum(rel, 0.0)[None] * log_g[:, None, None]), 0.0)
    xi = jnp.exp((idx + 1.0)[None] * log_g[:, None])
    zeta = jnp.exp((c - 1.0 - idx)[None] * log_g[:, None])
    g_chunk = jnp.exp(c * log_g)
    bcast = lambda t: jnp.broadcast_to(t[..., None], t.shape + (d,))
    return cos_full, sin_signed, decay, bcast(xi), bcast(zeta), bcast(g_chunk[:, None])


def _retention(z3d, ret_gn_gain):
    b, s, _ = z3d.shape
    c, d = RET_CHUNK, RET_HEAD_DIM
    base = 4 * ATTN_WIDTH // LANES
    cos, sin, decay, xi, zeta, gc = _retention_constants(s)
    gain = ret_gn_gain.astype(F32).reshape(RET_HEADS, 1, d)
    zspec = lambda off: pl.BlockSpec((1, c, d), lambda bi, h, ci: (bi, ci, base + off * RET_HEADS + h))
    per_head = lambda rows: pl.BlockSpec((1, rows, d), lambda bi, h, ci: (h, 0, 0))
    return pl.pallas_call(
        _retention_kernel,
        grid=(b, RET_HEADS, s // c),
        in_specs=[zspec(0), zspec(1), zspec(2), zspec(3),
                  pl.BlockSpec((c, d), lambda bi, h, ci: (ci, 0)),
                  pl.BlockSpec((c, d), lambda bi, h, ci: (ci, 0)),
                  pl.BlockSpec((1, c, c), lambda bi, h, ci: (h, 0, 0)),
                  per_head(c), per_head(c), per_head(1), per_head(1)],
        out_specs=pl.BlockSpec((1, c, d), lambda bi, h, ci: (bi, ci, h)),
        out_shape=jax.ShapeDtypeStruct((b, s, RET_WIDTH), BF16),
        scratch_shapes=[pltpu.VMEM((d, d), F32)],
        compiler_params=pltpu.CompilerParams(dimension_semantics=("arbitrary",) * 3,
                                             vmem_limit_bytes=VMEM_LIMIT),
        name="retention",
    )(z3d, z3d, z3d, z3d, cos, sin, decay, xi, zeta, gc, gain)


def _output_kernel(x_ref, a_ref, r_ref, p_ref, wa_ref, wr_ref, wg_ref, wp_ref, gain_ref, bias_ref, o_ref):
    mix = (jnp.dot(a_ref[...], wa_ref[...], preferred_element_type=F32)
           + jnp.dot(r_ref[...], wr_ref[...], preferred_element_type=F32))
    u = DEEPNORM_ALPHA * x_ref[...] + mix
    ple = jnp.dot(p_ref[...].astype(BF16), wp_ref[...], preferred_element_type=F32)
    gate = jax.nn.sigmoid(jnp.dot(u.astype(BF16), wg_ref[...], preferred_element_type=F32))
    u = u + gate * ple
    mu = jnp.mean(u, axis=-1, keepdims=True)
    cen = u - mu
    var = jnp.mean(cen * cen, axis=-1, keepdims=True)
    o_ref[...] = cen * lax.rsqrt(var + LN_EPS) * gain_ref[...] + bias_ref[...]


def _output_stage(x2d, attn2d, ret2d, p2d, w_out_bf16, w_gate_bf16, w_proj_bf16, ln_gain, ln_bias, *, tm=512):
    m, dm = x2d.shape
    rows = lambda width: pl.BlockSpec((tm, width), lambda i: (i, 0))
    whole = lambda shape: pl.BlockSpec(shape, lambda i: (0, 0))
    return pl.pallas_call(
        _output_kernel,
        grid=(m // tm,),
        in_specs=[rows(dm), rows(ATTN_WIDTH), rows(RET_WIDTH), rows(D_PLE),
                  pl.BlockSpec((ATTN_WIDTH, dm), lambda i: (0, 0)),
                  pl.BlockSpec((RET_WIDTH, dm), lambda i: (1, 0)),
                  whole((dm, dm)), whole((D_PLE, dm)), whole((1, dm)), whole((1, dm))],
        out_specs=rows(dm),
        out_shape=jax.ShapeDtypeStruct((m, dm), F32),
        compiler_params=pltpu.CompilerParams(dimension_semantics=("arbitrary",),
                                             vmem_limit_bytes=VMEM_LIMIT),
        name="output_stage",
    )(x2d, attn2d, ret2d, p2d, w_out_bf16, w_out_bf16, w_gate_bf16, w_proj_bf16,
      ln_gain.astype(F32).reshape(1, dm), ln_bias.astype(F32).reshape(1, dm))


def kernel(x, p, w_in, w_out, ret_gn_gain, w_ple_gate, w_ple_proj, ln_gain, ln_bias):
    b, s, dm = x.shape
    h = x
    for i in range(DEPTH):
        h2d = h.reshape(b * s, dm)
        z = _input_projection(h2d, w_in[i].astype(BF16))
        z3d = z.reshape(b, s, IN_WIDTH)
        attn = _moba_attention(z3d)
        ret = _retention(z3d, ret_gn_gain[i])
        out = _output_stage(h2d, attn.reshape(b * s, ATTN_WIDTH), ret.reshape(b * s, RET_WIDTH),
                            p[i].reshape(b * s, D_PLE), w_out[i].astype(BF16), w_ple_gate[i].astype(BF16),
                            w_ple_proj[i].astype(BF16), ln_gain[i], ln_bias[i])
        h = out.reshape(b, s, dm)
    return h
```

```python
import functools

import jax
import jax.numpy as jnp
from jax import lax
from jax.experimental import pallas as pl
from jax.experimental.pallas import tpu as pltpu

F32 = jnp.float32
BF16 = jnp.bfloat16

D_MODEL = 1024
D_PLE = 256
ATTN_HEADS = 8
ATTN_HEAD_DIM = 64
ATTN_WIDTH = ATTN_HEADS * ATTN_HEAD_DIM
RET_HEADS = 4
RET_HEAD_DIM = 128
RET_WIDTH = RET_HEADS * RET_HEAD_DIM
IN_WIDTH = 4 * ATTN_WIDTH + 4 * RET_WIDTH
MOBA_BLOCK = 256
MOBA_TOPK = 3
ROPE_BASE = 10000.0
DEPTH = 1
DEEPNORM_ALPHA = (2.0 * DEPTH) ** 0.25
LN_EPS = 1e-5
GN_EPS = 1e-6

LANES = 128
HEADS_PER_STEP = LANES // ATTN_HEAD_DIM
MASKED = -1e30
ONES_ROWS = 16
CHUNK_BLOCKS = 4
RET_CHUNK = 256
VMEM_LIMIT = 48 * 1024 * 1024


def _proj_kernel(x_ref, w_ref, o_ref, *, n_chunk):
    xb = x_ref[...].astype(BF16)
    for j in range(o_ref.shape[1] // n_chunk):
        cols = slice(j * n_chunk, (j + 1) * n_chunk)
        o_ref[:, cols] = jnp.dot(xb, w_ref[:, cols], preferred_element_type=F32).astype(o_ref.dtype)


def _input_projection(x2d, w_in_bf16, *, tm=512, n_chunk=1024):
    m, k = x2d.shape
    n = w_in_bf16.shape[1]
    return pl.pallas_call(
        functools.partial(_proj_kernel, n_chunk=n_chunk),
        grid=(m // tm,),
        in_specs=[pl.BlockSpec((tm, k), lambda i: (i, 0)),
                  pl.BlockSpec((k, n), lambda i: (0, 0))],
        out_specs=pl.BlockSpec((tm, n), lambda i: (i, 0)),
        out_shape=jax.ShapeDtypeStruct((m, n), BF16),
        compiler_params=pltpu.CompilerParams(dimension_semantics=("arbitrary",),
                                             vmem_limit_bytes=VMEM_LIMIT),
        name="input_projection",
    )(x2d, w_in_bf16)


def _moba_kernel(q_ref, k_ref, v_ref, g_ref, o_ref,
                 vtc_ref, qtb_ref, bias_ref, s_ref, m_ref, acc_ref, *, nb):
    blk = MOBA_BLOCK
    hd = ATTN_HEAD_DIM
    nh = HEADS_PER_STEP
    width = nh * blk
    vrows = LANES + ONES_ROWS
    seq = nb * blk

    def _prepare_head_pair():
        ones = jnp.ones((ONES_ROWS, blk), BF16)
        for n in range(nb):
            vt = v_ref[0, n * blk:(n + 1) * blk, :].astype(F32).T.astype(BF16)
            c, j = divmod(n, CHUNK_BLOCKS)
            vtc_ref[c, 0:LANES, j * blk:(j + 1) * blk] = vt
            vtc_ref[c, LANES:vrows, j * blk:(j + 1) * blk] = ones
        key_blk = lax.broadcasted_iota(jnp.int32, (nb, seq), 1) // blk
        row_blk = lax.broadcasted_iota(jnp.int32, (nb, seq), 0)
        averager = jnp.where(key_blk == row_blk, 1.0 / blk, 0.0).astype(BF16)
        kmean = jnp.dot(averager, k_ref[0], preferred_element_type=F32)
        hi = kmean.astype(BF16)
        lo = (kmean - hi.astype(F32)).astype(BF16)
        lane = lax.broadcasted_iota(jnp.int32, (nb, LANES), 1)
        parts = []
        for h in range(nh):
            in_head = (lane >= h * hd) & (lane < (h + 1) * hd)
            parts += [jnp.where(in_head, hi, jnp.zeros_like(hi)), jnp.where(in_head, lo, jnp.zeros_like(lo))]
        kstack = jnp.concatenate(parts, axis=0)
        feat = lax.broadcasted_iota(jnp.int32, (LANES, blk), 0)
        qts = []
        for n in range(nb):
            qt = (q_ref[0, n * blk:(n + 1) * blk, :].astype(F32) * (hd ** -0.5)).T.astype(BF16)
            for h in range(nh):
                in_head = (feat >= h * hd) & (feat < (h + 1) * hd)
                qtb_ref[n, :, h * blk:(h + 1) * blk] = jnp.where(in_head, qt, jnp.zeros_like(qt))
            qts.append(qt)
        gates = jnp.dot(kstack, jnp.concatenate(qts, axis=1), preferred_element_type=F32)
        past = row_blk < key_blk
        for h in range(nh):
            g = gates[(2 * h) * nb:(2 * h + 1) * nb] + gates[(2 * h + 1) * nb:(2 * h + 2) * nb]
            g = jnp.where(past, g, -jnp.inf)
            rank = jnp.zeros((nb, seq), F32)
            for m in range(nb):
                gm = g[m:m + 1, :]
                tie_first = (row_blk > m).astype(F32)
                rank = rank + jnp.where(gm > g, 1.0, jnp.where(gm == g, tie_first, 0.0))
            bias = jnp.where((rank < MOBA_TOPK) & past, 0.0, MASKED)
            for n in range(nb):
                bias_ref[n, :, h * blk:(h + 1) * blk] = bias[:, n * blk:(n + 1) * blk]

    _prepare_head_pair()

    key_pos = lax.broadcasted_iota(jnp.int32, (blk, width), 0)
    q_pos = lax.broadcasted_iota(jnp.int32, (blk, width), 1) & (blk - 1)
    causal = key_pos <= q_pos

    def block_rows(n0, cnt):
        return pl.ds(pl.multiple_of(n0 * blk, blk), cnt * blk)

    def query_block(qi, carry):
        qt_both = qtb_ref[qi]

        def score_blocks(n0, cnt, own_last=False):
            s = jnp.dot(k_ref[0, block_rows(n0, cnt), :], qt_both, preferred_element_type=F32)
            m8 = m_ref[...]
            for j in range(cnt):
                sj = s[j * blk:(j + 1) * blk]
                if own_last and j == cnt - 1:
                    sj = jnp.where(causal, sj, MASKED)
                else:
                    sj = sj + bias_ref[qi, pl.ds(n0 + j, 1), :]
                s_ref[block_rows(n0 + j, 1), :] = sj
                m8 = jnp.maximum(m8, jnp.max(sj.reshape(blk // 8, 8, width), axis=0))
            m_ref[...] = m8

        def value_blocks(n0, cnt, vt, m):
            p = jnp.exp(s_ref[block_rows(n0, cnt), :] - m).astype(BF16)
            acc_ref[...] += jnp.dot(vt, p, preferred_element_type=F32)

        n_plain = qi // CHUNK_BLOCKS
        tail = qi + 1 - n_plain * CHUNK_BLOCKS
        tail_start = n_plain * CHUNK_BLOCKS

        m_ref[...] = jnp.full(m_ref.shape, -jnp.inf, F32)

        def score_chunk(c, carry):
            score_blocks(c * CHUNK_BLOCKS, CHUNK_BLOCKS)
            return carry

        lax.fori_loop(0, n_plain, score_chunk, 0)
        for r in range(1, CHUNK_BLOCKS + 1):
            @pl.when(tail == r)
            def _score_tail(r=r):
                score_blocks(tail_start, r, own_last=True)

        m = jnp.max(m_ref[...], axis=0, keepdims=True)
        acc_ref[...] = jnp.zeros(acc_ref.shape, F32)

        def value_chunk(c, carry):
            value_blocks(c * CHUNK_BLOCKS, CHUNK_BLOCKS, vtc_ref[c], m)
            return carry

        lax.fori_loop(0, n_plain, value_chunk, 0)
        for r in range(1, CHUNK_BLOCKS + 1):
            @pl.when(tail == r)
            def _value_tail(r=r):
                value_blocks(tail_start, r, vtc_ref[n_plain, :, 0:r * blk], m)

        outs = []
        for h in range(nh):
            cols = slice(h * blk, (h + 1) * blk)
            outs.append(acc_ref[h * hd:(h + 1) * hd, cols] / acc_ref[LANES:LANES + 1, cols])
        attn = jnp.concatenate(outs, axis=0).T
        gate = g_ref[0, block_rows(qi, 1), :].astype(F32)
        o_ref[0, block_rows(qi, 1), :] = (attn * (gate * jax.nn.sigmoid(gate))).astype(o_ref.dtype)
        return carry

    lax.fori_loop(0, nb, query_block, 0)


def _moba_attention(z3d):
    b, s, _ = z3d.shape
    nb = s // MOBA_BLOCK
    steps = ATTN_WIDTH // LANES
    width = HEADS_PER_STEP * MOBA_BLOCK
    vrows = LANES + ONES_ROWS
    kernel = functools.partial(_moba_kernel, nb=nb)
    full = lambda off: pl.BlockSpec((1, s, LANES), lambda bi, hp: (bi, 0, off * steps + hp))
    return pl.pallas_call(
        kernel,
        grid=(b, steps),
        in_specs=[full(0), full(1), full(2), full(3)],
        out_specs=pl.BlockSpec((1, s, LANES), lambda bi, hp: (bi, 0, hp)),
        out_shape=jax.ShapeDtypeStruct((b, s, ATTN_WIDTH), BF16),
        scratch_shapes=[pltpu.VMEM((nb // CHUNK_BLOCKS, vrows, CHUNK_BLOCKS * MOBA_BLOCK), BF16),
                        pltpu.VMEM((nb, LANES, width), BF16),
                        pltpu.VMEM((nb, nb, width), F32),
                        pltpu.VMEM((s, width), F32),
                        pltpu.VMEM((8, width), F32),
                        pltpu.VMEM((vrows, width), F32)],
        compiler_params=pltpu.CompilerParams(dimension_semantics=("arbitrary",) * 2,
                                             vmem_limit_bytes=VMEM_LIMIT),
        name="moba_attention",
    )(z3d, z3d, z3d, z3d)


def _retention_kernel(q_ref, k_ref, v_ref, g_ref, cos_ref, sin_ref, decay_ref, xi_ref, zeta_ref,
                      gc_ref, gain_ref, o_ref, state_ref):
    c = pl.program_id(1)
    d = RET_HEAD_DIM

    @pl.when(c == 0)
    def _reset_state():
        state_ref[...] = jnp.zeros_like(state_ref)

    cos = cos_ref[...]
    sin = sin_ref[...]

    def rotate(t):
        return t * cos + pltpu.roll(t, d // 2, 1) * sin

    for h in range(RET_HEADS):
        cols = slice(h * d, (h + 1) * d)
        qr = rotate(q_ref[0, :, cols].astype(F32))
        kr = rotate(k_ref[0, :, cols].astype(F32)) * (d ** -0.5)
        qb = qr.astype(BF16)
        vb = v_ref[0, :, cols]
        scores = lax.dot_general(qb, kr.astype(BF16), (((1,), (1,)), ((), ())),
                                 preferred_element_type=F32)
        inner = (scores * decay_ref[h]).astype(BF16)
        state = state_ref[h]
        out = (jnp.dot(inner, vb, preferred_element_type=F32)
               + jnp.dot(qb, state.astype(BF16), preferred_element_type=F32) * xi_ref[h])
        kz_t = (kr * zeta_ref[h]).T.astype(BF16)
        state_ref[h] = state * gc_ref[h] + jnp.dot(kz_t, vb, preferred_element_type=F32)

        mu = jnp.mean(out, axis=-1, keepdims=True)
        cen = out - mu
        var = jnp.mean(cen * cen, axis=-1, keepdims=True)
        normed = cen * lax.rsqrt(var + GN_EPS) * gain_ref[h]
        gate = g_ref[0, :, cols].astype(F32)
        o_ref[0, :, cols] = (normed * (gate * jax.nn.sigmoid(gate))).astype(o_ref.dtype)


def _retention_constants(s):
    d, c = RET_HEAD_DIM, RET_CHUNK
    half = d // 2
    freqs = ROPE_BASE ** (-jnp.arange(half, dtype=F32) / half)
    ang = jnp.arange(s).astype(F32)[:, None] * freqs[None, :]
    cos, sin = jnp.cos(ang), jnp.sin(ang)
    cos_full = jnp.concatenate([cos, cos], axis=-1)
    sin_signed = jnp.concatenate([-sin, sin], axis=-1)
    log_g = jnp.log1p(-jnp.exp2(-5.0 - jnp.arange(RET_HEADS, dtype=F32)))
    idx = jnp.arange(c, dtype=F32)
    rel = idx[:, None] - idx[None, :]
    decay = jnp.where(rel >= 0, jnp.exp(jnp.maximum(rel, 0.0)[None] * log_g[:, None, None]), 0.0)
    xi = jnp.exp((idx + 1.0)[None] * log_g[:, None])
    zeta = jnp.exp((c - 1.0 - idx)[None] * log_g[:, None])
    g_chunk = jnp.exp(c * log_g)
    bcast = lambda t: jnp.broadcast_to(t[..., None], t.shape + (d,))
    return cos_full, sin_signed, decay, bcast(xi), bcast(zeta), bcast(g_chunk[:, None])


def _retention(z3d, ret_gn_gain):
    b, s, _ = z3d.shape
    c, d = RET_CHUNK, RET_HEAD_DIM
    base = 4 * ATTN_WIDTH // RET_WIDTH
    cos, sin, decay, xi, zeta, gc = _retention_constants(s)
    gain = ret_gn_gain.astype(F32).reshape(RET_HEADS, 1, d)
    zspec = lambda off: pl.BlockSpec((1, c, RET_WIDTH), lambda bi, ci: (bi, ci, base + off))
    whole = lambda *shape: pl.BlockSpec(shape, lambda bi, ci: (0,) * len(shape))
    return pl.pallas_call(
        _retention_kernel,
        grid=(b, s // c),
        in_specs=[zspec(0), zspec(1), zspec(2), zspec(3),
                  pl.BlockSpec((c, d), lambda bi, ci: (ci, 0)),
                  pl.BlockSpec((c, d), lambda bi, ci: (ci, 0)),
                  whole(RET_HEADS, c, c), whole(RET_HEADS, c, d), whole(RET_HEADS, c, d),
                  whole(RET_HEADS, 1, d), whole(RET_HEADS, 1, d)],
        out_specs=pl.BlockSpec((1, c, RET_WIDTH), lambda bi, ci: (bi, ci, 0)),
        out_shape=jax.ShapeDtypeStruct((b, s, RET_WIDTH), BF16),
        scratch_shapes=[pltpu.VMEM((RET_HEADS, d, d), F32)],
        compiler_params=pltpu.CompilerParams(dimension_semantics=("arbitrary",) * 2,
                                             vmem_limit_bytes=VMEM_LIMIT),
        name="retention",
    )(z3d, z3d, z3d, z3d, cos, sin, decay, xi, zeta, gc, gain)


def _output_kernel(x_ref, a_ref, r_ref, p_ref, wa_ref, wr_ref, wg_ref, wp_ref, gain_ref, bias_ref, o_ref):
    mix = (jnp.dot(a_ref[...], wa_ref[...], preferred_element_type=F32)
           + jnp.dot(r_ref[...], wr_ref[...], preferred_element_type=F32))
    u = DEEPNORM_ALPHA * x_ref[...] + mix
    ple = jnp.dot(p_ref[...].astype(BF16), wp_ref[...], preferred_element_type=F32)
    gate = jax.nn.sigmoid(jnp.dot(u.astype(BF16), wg_ref[...], preferred_element_type=F32))
    u = u + gate * ple
    mu = jnp.mean(u, axis=-1, keepdims=True)
    cen = u - mu
    var = jnp.mean(cen * cen, axis=-1, keepdims=True)
    o_ref[...] = cen * lax.rsqrt(var + LN_EPS) * gain_ref[...] + bias_ref[...]


def _output_stage(x2d, attn2d, ret2d, p2d, w_out_bf16, w_gate_bf16, w_proj_bf16, ln_gain, ln_bias, *, tm=512):
    m, dm = x2d.shape
    rows = lambda width: pl.BlockSpec((tm, width), lambda i: (i, 0))
    whole = lambda shape: pl.BlockSpec(shape, lambda i: (0, 0))
    return pl.pallas_call(
        _output_kernel,
        grid=(m // tm,),
        in_specs=[rows(dm), rows(ATTN_WIDTH), rows(RET_WIDTH), rows(D_PLE),
                  pl.BlockSpec((ATTN_WIDTH, dm), lambda i: (0, 0)),
                  pl.BlockSpec((RET_WIDTH, dm), lambda i: (1, 0)),
                  whole((dm, dm)), whole((D_PLE, dm)), whole((1, dm)), whole((1, dm))],
        out_specs=rows(dm),
        out_shape=jax.ShapeDtypeStruct((m, dm), F32),
        compiler_params=pltpu.CompilerParams(dimension_semantics=("arbitrary",),
                                             vmem_limit_bytes=VMEM_LIMIT),
        name="output_stage",
    )(x2d, attn2d, ret2d, p2d, w_out_bf16, w_out_bf16, w_gate_bf16, w_proj_bf16,
      ln_gain.astype(F32).reshape(1, dm), ln_bias.astype(F32).reshape(1, dm))


def kernel(x, p, w_in, w_out, ret_gn_gain, w_ple_gate, w_ple_proj, ln_gain, ln_bias):
    b, s, dm = x.shape
    h = x
    for i in range(DEPTH):
        h2d = h.reshape(b * s, dm)
        z = _input_projection(h2d, w_in[i].astype(BF16))
        z3d = z.reshape(b, s, IN_WIDTH)
        attn = _moba_attention(z3d)
        ret = _retention(z3d, ret_gn_gain[i])
        out = _output_stage(h2d, attn.reshape(b * s, ATTN_WIDTH), ret.reshape(b * s, RET_WIDTH),
                            p[i].reshape(b * s, D_PLE), w_out[i].astype(BF16), w_ple_gate[i].astype(BF16),
                            w_ple_proj[i].astype(BF16), ln_gain[i], ln_bias[i])
        h = out.reshape(b, s, dm)
    return h
```

```python
import functools

import jax
import jax.numpy as jnp
from jax import lax
from jax.experimental import pallas as pl
from jax.experimental.pallas import tpu as pltpu

F32 = jnp.float32
BF16 = jnp.bfloat16

D_MODEL = 1024
D_PLE = 256
ATTN_HEADS = 8
ATTN_HEAD_DIM = 64
ATTN_WIDTH = ATTN_HEADS * ATTN_HEAD_DIM
RET_HEADS = 4
RET_HEAD_DIM = 128
RET_WIDTH = RET_HEADS * RET_HEAD_DIM
IN_WIDTH = 4 * ATTN_WIDTH + 4 * RET_WIDTH
MOBA_BLOCK = 256
MOBA_TOPK = 3
ROPE_BASE = 10000.0
DEPTH = 1
DEEPNORM_ALPHA = (2.0 * DEPTH) ** 0.25
LN_EPS = 1e-5
GN_EPS = 1e-6

LANES = 128
HEADS_PER_STEP = LANES // ATTN_HEAD_DIM
MASKED = -1e30
LOWEST = -3e38
LOG2_E = 1.4426950408889634
ONES_ROWS = 16
CHUNK_BLOCKS = 4
RET_CHUNK = 256
VMEM_LIMIT = 48 * 1024 * 1024


def _proj_kernel(x_ref, w_ref, o_ref, *, n_chunk):
    xb = x_ref[...].astype(BF16)
    for j in range(o_ref.shape[1] // n_chunk):
        cols = slice(j * n_chunk, (j + 1) * n_chunk)
        o_ref[:, cols] = jnp.dot(xb, w_ref[:, cols], preferred_element_type=F32).astype(o_ref.dtype)


def _input_projection(x2d, w_in_bf16, *, tm=512, n_chunk=1024):
    m, k = x2d.shape
    n = w_in_bf16.shape[1]
    return pl.pallas_call(
        functools.partial(_proj_kernel, n_chunk=n_chunk),
        grid=(m // tm,),
        in_specs=[pl.BlockSpec((tm, k), lambda i: (i, 0)),
                  pl.BlockSpec((k, n), lambda i: (0, 0))],
        out_specs=pl.BlockSpec((tm, n), lambda i: (i, 0)),
        out_shape=jax.ShapeDtypeStruct((m, n), BF16),
        compiler_params=pltpu.CompilerParams(dimension_semantics=("arbitrary",),
                                             vmem_limit_bytes=VMEM_LIMIT),
        name="input_projection",
    )(x2d, w_in_bf16)


def _moba_kernel(q_ref, k_ref, v_ref, g_ref, o_ref,
                 kx_ref, vtc_ref, qtb_ref, s0_ref, s1_ref, mrun_ref, mnow_ref, alpha_ref, acc_ref, *, nb):
    blk = MOBA_BLOCK
    hd = ATTN_HEAD_DIM
    nh = HEADS_PER_STEP
    width = nh * blk
    vrows = LANES + ONES_ROWS
    seq = nb * blk
    chunk = CHUNK_BLOCKS * blk
    n_chunks = nb // CHUNK_BLOCKS
    s_refs = (s0_ref, s1_ref)

    def _prepare_head_pair():
        ones = jnp.ones((ONES_ROWS, blk), BF16)
        for n in range(nb):
            vt = v_ref[0, n * blk:(n + 1) * blk, :].astype(F32).T.astype(BF16)
            c, j = divmod(n, CHUNK_BLOCKS)
            vtc_ref[c, 0:LANES, j * blk:(j + 1) * blk] = vt
            vtc_ref[c, LANES:vrows, j * blk:(j + 1) * blk] = ones
        kx_ref[:, 0:LANES] = k_ref[0]
        key_row_blk = lax.broadcasted_iota(jnp.int32, (seq, LANES), 0) // blk
        kx_ref[:, LANES:2 * LANES] = jnp.where(
            key_row_blk == lax.broadcasted_iota(jnp.int32, (seq, LANES), 1), 1.0, 0.0).astype(BF16)
        key_blk = lax.broadcasted_iota(jnp.int32, (nb, seq), 1) // blk
        row_blk = lax.broadcasted_iota(jnp.int32, (nb, seq), 0)
        averager = jnp.where(key_blk == row_blk, 1.0 / blk, 0.0).astype(BF16)
        kmean = jnp.dot(averager, k_ref[0], preferred_element_type=F32)
        hi = kmean.astype(BF16)
        lo = (kmean - hi.astype(F32)).astype(BF16)
        lane = lax.broadcasted_iota(jnp.int32, (nb, LANES), 1)
        parts = []
        for h in range(nh):
            in_head = (lane >= h * hd) & (lane < (h + 1) * hd)
            parts += [jnp.where(in_head, hi, jnp.zeros_like(hi)), jnp.where(in_head, lo, jnp.zeros_like(lo))]
        kstack = jnp.concatenate(parts, axis=0)
        feat = lax.broadcasted_iota(jnp.int32, (LANES, blk), 0)
        qts = []
        for n in range(nb):
            qt = (q_ref[0, n * blk:(n + 1) * blk, :].astype(F32) * (hd ** -0.5 * LOG2_E)).T.astype(BF16)
            for h in range(nh):
                in_head = (feat >= h * hd) & (feat < (h + 1) * hd)
                qtb_ref[n, 0:LANES, h * blk:(h + 1) * blk] = jnp.where(in_head, qt, jnp.zeros_like(qt))
            qtb_ref[n, LANES + nb:2 * LANES, :] = jnp.zeros((LANES - nb, width), BF16)
            qts.append(qt)
        gates = jnp.dot(kstack, jnp.concatenate(qts, axis=1), preferred_element_type=F32)
        past = row_blk < key_blk
        for h in range(nh):
            g = gates[(2 * h) * nb:(2 * h + 1) * nb] + gates[(2 * h + 1) * nb:(2 * h + 2) * nb]
            g = jnp.where(past, g, -jnp.inf)
            rank = jnp.zeros((nb, seq), F32)
            for m in range(nb):
                gm = g[m:m + 1, :]
                tie_first = (row_blk > m).astype(F32)
                rank = rank + jnp.where(gm > g, 1.0, jnp.where(gm == g, tie_first, 0.0))
            bias = jnp.where(((rank < MOBA_TOPK) & past) | (row_blk == key_blk), 0.0, MASKED).astype(BF16)
            for n in range(nb):
                qtb_ref[n, LANES:LANES + nb, h * blk:(h + 1) * blk] = bias[:, n * blk:(n + 1) * blk]

    _prepare_head_pair()

    key_pos = lax.broadcasted_iota(jnp.int32, (blk, width), 0)
    q_pos = lax.broadcasted_iota(jnp.int32, (blk, width), 1) & (blk - 1)
    causal = key_pos <= q_pos

    def rows_at(start, size):
        return pl.ds(pl.multiple_of(start, blk), size)

    def scores(c, qi, slot, cnt, own_j=None):
        s = jnp.dot(kx_ref[rows_at(c * chunk, cnt * blk), :], qtb_ref[qi], preferred_element_type=F32)
        m8 = None
        for j in range(cnt):
            sj = s[j * blk:(j + 1) * blk]
            if j == own_j:
                sj = jnp.where(causal, sj, MASKED)
            s_refs[slot][j * blk:(j + 1) * blk, :] = sj
            mj = jnp.max(sj.reshape(blk // 8, 8, width), axis=0)
            m8 = mj if m8 is None else jnp.maximum(m8, mj)
        m_old = mrun_ref[qi]
        m_new = jnp.maximum(m_old, jnp.max(m8, axis=0, keepdims=True))
        mrun_ref[qi] = m_new
        mnow_ref[slot] = m_new
        alpha_ref[slot] = jnp.exp2(m_old - m_new)

    def values(c, qi, slot, cnt):
        p = jnp.exp2(s_refs[slot][0:cnt * blk, :] - mnow_ref[slot, 0:1, :]).astype(BF16)
        pv = jnp.dot(vtc_ref[c, :, 0:cnt * blk], p, preferred_element_type=F32)
        acc_ref[qi] = acc_ref[qi] * alpha_ref[slot, 0:1, :] + pv

    def finish(qi):
        outs = []
        for h in range(nh):
            cols = slice(h * blk, (h + 1) * blk)
            outs.append(acc_ref[qi, h * hd:(h + 1) * hd, cols] / acc_ref[qi, LANES:LANES + 1, cols])
        attn = jnp.concatenate(outs, axis=0).T
        gate = g_ref[0, rows_at(qi * blk, blk), :].astype(F32)
        o_ref[0, rows_at(qi * blk, blk), :] = (attn * (gate * jax.nn.sigmoid(gate))).astype(o_ref.dtype)

    mrun_ref[...] = jnp.full(mrun_ref.shape, LOWEST, F32)
    acc_ref[...] = jnp.zeros(acc_ref.shape, F32)

    def key_chunk(c, carry):
        first = c * CHUNK_BLOCKS
        for j in range(CHUNK_BLOCKS):
            scores(c, first + j, j % 2, j + 1, own_j=j)
            values(c, first + j, j % 2, j + 1)
            finish(first + j)

        later = first + CHUNK_BLOCKS
        n_pairs = (nb - later) // 2

        @pl.when(n_pairs > 0)
        def _prime():
            scores(c, later, 0, CHUNK_BLOCKS)

        def query_pair(t, carry):
            qa = later + 2 * t
            scores(c, qa + 1, 1, CHUNK_BLOCKS)
            values(c, qa, 0, CHUNK_BLOCKS)
            scores(c, jnp.minimum(qa + 2, nb - 1), 0, CHUNK_BLOCKS)
            values(c, qa + 1, 1, CHUNK_BLOCKS)
            return carry

        lax.fori_loop(0, n_pairs, query_pair, 0)
        return carry

    lax.fori_loop(0, n_chunks, key_chunk, 0)


def _moba_attention(z3d):
    b, s, _ = z3d.shape
    nb = s // MOBA_BLOCK
    steps = ATTN_WIDTH // LANES
    width = HEADS_PER_STEP * MOBA_BLOCK
    vrows = LANES + ONES_ROWS
    kernel = functools.partial(_moba_kernel, nb=nb)
    full = lambda off: pl.BlockSpec((1, s, LANES), lambda bi, hp: (bi, 0, off * steps + hp))
    return pl.pallas_call(
        kernel,
        grid=(b, steps),
        in_specs=[full(0), full(1), full(2), full(3)],
        out_specs=pl.BlockSpec((1, s, LANES), lambda bi, hp: (bi, 0, hp)),
        out_shape=jax.ShapeDtypeStruct((b, s, ATTN_WIDTH), BF16),
        scratch_shapes=[pltpu.VMEM((s, 2 * LANES), BF16),
                        pltpu.VMEM((nb // CHUNK_BLOCKS, vrows, CHUNK_BLOCKS * MOBA_BLOCK), BF16),
                        pltpu.VMEM((nb, 2 * LANES, width), BF16),
                        pltpu.VMEM((CHUNK_BLOCKS * MOBA_BLOCK, width), F32),
                        pltpu.VMEM((CHUNK_BLOCKS * MOBA_BLOCK, width), F32),
                        pltpu.VMEM((nb, 8, width), F32),
                        pltpu.VMEM((2, 8, width), F32),
                        pltpu.VMEM((2, 8, width), F32),
                        pltpu.VMEM((nb, vrows, width), F32)],
        compiler_params=pltpu.CompilerParams(dimension_semantics=("arbitrary",) * 2,
                                             vmem_limit_bytes=VMEM_LIMIT),
        name="moba_attention",
    )(z3d, z3d, z3d, z3d)


def _retention_kernel(q_ref, k_ref, v_ref, g_ref, cos_ref, sin_ref, decay_ref, xi_ref, zeta_ref,
                      gc_ref, gain_ref, o_ref, state_ref):
    c = pl.program_id(1)
    d = RET_HEAD_DIM

    @pl.when(c == 0)
    def _reset_state():
        state_ref[...] = jnp.zeros_like(state_ref)

    cos = cos_ref[...]
    sin = sin_ref[...]

    def rotate(t):
        return t * cos + pltpu.roll(t, d // 2, 1) * sin

    for h in range(RET_HEADS):
        cols = slice(h * d, (h + 1) * d)
        qr = rotate(q_ref[0, :, cols].astype(F32))
        kr = rotate(k_ref[0, :, cols].astype(F32)) * (d ** -0.5)
        qb = qr.astype(BF16)
        vb = v_ref[0, :, cols]
        scores = lax.dot_general(qb, kr.astype(BF16), (((1,), (1,)), ((), ())),
                                 preferred_element_type=F32)
        inner = (scores * decay_ref[h]).astype(BF16)
        state = state_ref[h]
        out = (jnp.dot(inner, vb, preferred_element_type=F32)
               + jnp.dot(qb, state.astype(BF16), preferred_element_type=F32) * xi_ref[h])
        kz_t = (kr * zeta_ref[h]).T.astype(BF16)
        state_ref[h] = state * gc_ref[h] + jnp.dot(kz_t, vb, preferred_element_type=F32)

        mu = jnp.mean(out, axis=-1, keepdims=True)
        cen = out - mu
        var = jnp.mean(cen * cen, axis=-1, keepdims=True)
        normed = cen * lax.rsqrt(var + GN_EPS) * gain_ref[h]
        gate = g_ref[0, :, cols].astype(F32)
        o_ref[0, :, cols] = (normed * (gate * jax.nn.sigmoid(gate))).astype(o_ref.dtype)


def _retention_constants(s):
    d, c = RET_HEAD_DIM, RET_CHUNK
    half = d // 2
    freqs = ROPE_BASE ** (-jnp.arange(half, dtype=F32) / half)
    ang = jnp.arange(s).astype(F32)[:, None] * freqs[None, :]
    cos, sin = jnp.cos(ang), jnp.sin(ang)
    cos_full = jnp.concatenate([cos, cos], axis=-1)
    sin_signed = jnp.concatenate([-sin, sin], axis=-1)
    log_g = jnp.log1p(-jnp.exp2(-5.0 - jnp.arange(RET_HEADS, dtype=F32)))
    idx = jnp.arange(c, dtype=F32)
    rel = idx[:, None] - idx[None, :]
    decay = jnp.where(rel >= 0, jnp.exp(jnp.maximum(rel, 0.0)[None] * log_g[:, None, None]), 0.0)
    xi = jnp.exp((idx + 1.0)[None] * log_g[:, None])
    zeta = jnp.exp((c - 1.0 - idx)[None] * log_g[:, None])
    g_chunk = jnp.exp(c * log_g)
    bcast = lambda t: jnp.broadcast_to(t[..., None], t.shape + (d,))
    return cos_full, sin_signed, decay, bcast(xi), bcast(zeta), bcast(g_chunk[:, None])


def _retention(z3d, ret_gn_gain):
    b, s, _ = z3d.shape
    c, d = RET_CHUNK, RET_HEAD_DIM
    base = 4 * ATTN_WIDTH // RET_WIDTH
    cos, sin, decay, xi, zeta, gc = _retention_constants(s)
    gain = ret_gn_gain.astype(F32).reshape(RET_HEADS, 1, d)
    zspec = lambda off: pl.BlockSpec((1, c, RET_WIDTH), lambda bi, ci: (bi, ci, base + off))
    whole = lambda *shape: pl.BlockSpec(shape, lambda bi, ci: (0,) * len(shape))
    return pl.pallas_call(
        _retention_kernel,
        grid=(b, s // c),
        in_specs=[zspec(0), zspec(1), zspec(2), zspec(3),
                  pl.BlockSpec((c, d), lambda bi, ci: (ci, 0)),
                  pl.BlockSpec((c, d), lambda bi, ci: (ci, 0)),
                  whole(RET_HEADS, c, c), whole(RET_HEADS, c, d), whole(RET_HEADS, c, d),
                  whole(RET_HEADS, 1, d), whole(RET_HEADS, 1, d)],
        out_specs=pl.BlockSpec((1, c, RET_WIDTH), lambda bi, ci: (bi, ci, 0)),
        out_shape=jax.ShapeDtypeStruct((b, s, RET_WIDTH), BF16),
        scratch_shapes=[pltpu.VMEM((RET_HEADS, d, d), F32)],
        compiler_params=pltpu.CompilerParams(dimension_semantics=("arbitrary",) * 2,
                                             vmem_limit_bytes=VMEM_LIMIT),
        name="retention",
    )(z3d, z3d, z3d, z3d, cos, sin, decay, xi, zeta, gc, gain)


def _output_kernel(x_ref, a_ref, r_ref, p_ref, wa_ref, wr_ref, wg_ref, wp_ref, gain_ref, bias_ref, o_ref):
    mix = (jnp.dot(a_ref[...], wa_ref[...], preferred_element_type=F32)
           + jnp.dot(r_ref[...], wr_ref[...], preferred_element_type=F32))
    u = DEEPNORM_ALPHA * x_ref[...] + mix
    ple = jnp.dot(p_ref[...].astype(BF16), wp_ref[...], preferred_element_type=F32)
    gate = jax.nn.sigmoid(jnp.dot(u.astype(BF16), wg_ref[...], preferred_element_type=F32))
    u = u + gate * ple
    mu = jnp.mean(u, axis=-1, keepdims=True)
    cen = u - mu
    var = jnp.mean(cen * cen, axis=-1, keepdims=True)
    o_ref[...] = cen * lax.rsqrt(var + LN_EPS) * gain_ref[...] + bias_ref[...]


def _output_stage(x2d, attn2d, ret2d, p2d, w_out_bf16, w_gate_bf16, w_proj_bf16, ln_gain, ln_bias, *, tm=512):
    m, dm = x2d.shape
    rows = lambda width: pl.BlockSpec((tm, width), lambda i: (i, 0))
    whole = lambda shape: pl.BlockSpec(shape, lambda i: (0, 0))
    return pl.pallas_call(
        _output_kernel,
        grid=(m // tm,),
        in_specs=[rows(dm), rows(ATTN_WIDTH), rows(RET_WIDTH), rows(D_PLE),
                  pl.BlockSpec((ATTN_WIDTH, dm), lambda i: (0, 0)),
                  pl.BlockSpec((RET_WIDTH, dm), lambda i: (1, 0)),
                  whole((dm, dm)), whole((D_PLE, dm)), whole((1, dm)), whole((1, dm))],
        out_specs=rows(dm),
        out_shape=jax.ShapeDtypeStruct((m, dm), F32),
        compiler_params=pltpu.CompilerParams(dimension_semantics=("arbitrary",),
                                             vmem_limit_bytes=VMEM_LIMIT),
        name="output_stage",
    )(x2d, attn2d, ret2d, p2d, w_out_bf16, w_out_bf16, w_gate_bf16, w_proj_bf16,
      ln_gain.astype(F32).reshape(1, dm), ln_bias.astype(F32).reshape(1, dm))


def kernel(x, p, w_in, w_out, ret_gn_gain, w_ple_gate, w_ple_proj, ln_gain, ln_bias):
    b, s, dm = x.shape
    h = x
    for i in range(DEPTH):
        h2d = h.reshape(b * s, dm)
        z = _input_projection(h2d, w_in[i].astype(BF16))
        z3d = z.reshape(b, s, IN_WIDTH)
        attn = _moba_attention(z3d)
        ret = _retention(z3d, ret_gn_gain[i])
        out = _output_stage(h2d, attn.reshape(b * s, ATTN_WIDTH), ret.reshape(b * s, RET_WIDTH),
                            p[i].reshape(b * s, D_PLE), w_out[i].astype(BF16), w_ple_gate[i].astype(BF16),
                            w_ple_proj[i].astype(BF16), ln_gain[i], ln_bias[i])
        h = out.reshape(b, s, dm)
    return h
```

```python
import functools

import jax
import jax.numpy as jnp
from jax import lax
from jax.experimental import pallas as pl
from jax.experimental.pallas import tpu as pltpu

F32 = jnp.float32
BF16 = jnp.bfloat16

D_MODEL = 1024
D_PLE = 256
ATTN_HEADS = 8
ATTN_HEAD_DIM = 64
ATTN_WIDTH = ATTN_HEADS * ATTN_HEAD_DIM
RET_HEADS = 4
RET_HEAD_DIM = 128
RET_WIDTH = RET_HEADS * RET_HEAD_DIM
IN_WIDTH = 4 * ATTN_WIDTH + 4 * RET_WIDTH
MOBA_BLOCK = 256
MOBA_TOPK = 3
ROPE_BASE = 10000.0
DEPTH = 1
DEEPNORM_ALPHA = (2.0 * DEPTH) ** 0.25
LN_EPS = 1e-5
GN_EPS = 1e-6

LANES = 128
HEADS_PER_STEP = LANES // ATTN_HEAD_DIM
MASKED = -1e30
LOWEST = -3e38
LOG2_E = 1.4426950408889634
ONES_ROWS = 16
CHUNK_BLOCKS = 4
RET_CHUNK = 256
VMEM_LIMIT = 48 * 1024 * 1024


def _proj_kernel(x_ref, w_ref, cos_ref, sin_ref, o_ref, *, n_chunk):
    xb = x_ref[...].astype(BF16)
    d = RET_HEAD_DIM
    rot_lo = 4 * ATTN_WIDTH
    rot_mid = rot_lo + RET_WIDTH
    rot_hi = rot_mid + RET_WIDTH
    assert rot_lo % n_chunk == 0 and rot_hi % n_chunk == 0 and n_chunk % d == 0
    for lo in range(0, o_ref.shape[1], n_chunk):
        acc = jnp.dot(xb, w_ref[:, lo:lo + n_chunk], preferred_element_type=F32)
        if rot_lo <= lo < rot_hi:
            cos = cos_ref[...]
            sin = sin_ref[...]
            for col in range(lo, lo + n_chunk, d):
                head = acc[:, col - lo:col - lo + d]
                rotated = head * cos + pltpu.roll(head, d // 2, 1) * sin
                if col >= rot_mid:
                    rotated = rotated * (d ** -0.5)
                o_ref[:, col:col + d] = rotated.astype(o_ref.dtype)
        else:
            o_ref[:, lo:lo + n_chunk] = acc.astype(o_ref.dtype)


def _rotary_tables(s):
    half = RET_HEAD_DIM // 2
    freqs = ROPE_BASE ** (-jnp.arange(half, dtype=F32) / half)
    ang = jnp.arange(s).astype(F32)[:, None] * freqs[None, :]
    cos, sin = jnp.cos(ang), jnp.sin(ang)
    return jnp.concatenate([cos, cos], axis=-1), jnp.concatenate([-sin, sin], axis=-1)


def _input_projection(x2d, w_in_bf16, seq, *, tm=512, n_chunk=1024):
    m, k = x2d.shape
    n = w_in_bf16.shape[1]
    cos, sin = _rotary_tables(seq)
    pos = pl.BlockSpec((tm, RET_HEAD_DIM), lambda i: (i % (seq // tm), 0))
    return pl.pallas_call(
        functools.partial(_proj_kernel, n_chunk=n_chunk),
        grid=(m // tm,),
        in_specs=[pl.BlockSpec((tm, k), lambda i: (i, 0)),
                  pl.BlockSpec((k, n), lambda i: (0, 0)),
                  pos, pos],
        out_specs=pl.BlockSpec((tm, n), lambda i: (i, 0)),
        out_shape=jax.ShapeDtypeStruct((m, n), BF16),
        compiler_params=pltpu.CompilerParams(dimension_semantics=("arbitrary",),
                                             vmem_limit_bytes=VMEM_LIMIT),
        name="input_projection",
    )(x2d, w_in_bf16, cos, sin)


def _moba_kernel(q_ref, k_ref, v_ref, g_ref, o_ref,
                 kx_ref, vtc_ref, qtb_ref, s0_ref, s1_ref, mrun_ref, mnow_ref, alpha_ref, acc_ref, *, nb):
    blk = MOBA_BLOCK
    hd = ATTN_HEAD_DIM
    nh = HEADS_PER_STEP
    width = nh * blk
    vrows = LANES + ONES_ROWS
    seq = nb * blk
    chunk = CHUNK_BLOCKS * blk
    n_chunks = nb // CHUNK_BLOCKS
    s_refs = (s0_ref, s1_ref)

    def _prepare_head_pair():
        ones = jnp.ones((ONES_ROWS, blk), BF16)
        for n in range(nb):
            vt = v_ref[0, n * blk:(n + 1) * blk, :].astype(F32).T.astype(BF16)
            c, j = divmod(n, CHUNK_BLOCKS)
            vtc_ref[c, 0:LANES, j * blk:(j + 1) * blk] = vt
            vtc_ref[c, LANES:vrows, j * blk:(j + 1) * blk] = ones
        kx_ref[:, 0:LANES] = k_ref[0]
        key_row_blk = lax.broadcasted_iota(jnp.int32, (seq, LANES), 0) // blk
        kx_ref[:, LANES:2 * LANES] = jnp.where(
            key_row_blk == lax.broadcasted_iota(jnp.int32, (seq, LANES), 1), 1.0, 0.0).astype(BF16)
        key_blk = lax.broadcasted_iota(jnp.int32, (nb, seq), 1) // blk
        row_blk = lax.broadcasted_iota(jnp.int32, (nb, seq), 0)
        averager = jnp.where(key_blk == row_blk, 1.0 / blk, 0.0).astype(BF16)
        kmean = jnp.dot(averager, k_ref[0], preferred_element_type=F32)
        hi = kmean.astype(BF16)
        lo = (kmean - hi.astype(F32)).astype(BF16)
        lane = lax.broadcasted_iota(jnp.int32, (nb, LANES), 1)
        parts = []
        for h in range(nh):
            in_head = (lane >= h * hd) & (lane < (h + 1) * hd)
            parts += [jnp.where(in_head, hi, jnp.zeros_like(hi)), jnp.where(in_head, lo, jnp.zeros_like(lo))]
        kstack = jnp.concatenate(parts, axis=0)
        feat = lax.broadcasted_iota(jnp.int32, (LANES, blk), 0)
        qts = []
        for n in range(nb):
            qt = (q_ref[0, n * blk:(n + 1) * blk, :].astype(F32) * (hd ** -0.5 * LOG2_E)).T.astype(BF16)
            for h in range(nh):
                in_head = (feat >= h * hd) & (feat < (h + 1) * hd)
                qtb_ref[n, 0:LANES, h * blk:(h + 1) * blk] = jnp.where(in_head, qt, jnp.zeros_like(qt))
            qtb_ref[n, LANES + nb:2 * LANES, :] = jnp.zeros((LANES - nb, width), BF16)
            qts.append(qt)
        gates = jnp.dot(kstack, jnp.concatenate(qts, axis=1), preferred_element_type=F32)
        past = row_blk < key_blk
        for h in range(nh):
            g = gates[(2 * h) * nb:(2 * h + 1) * nb] + gates[(2 * h + 1) * nb:(2 * h + 2) * nb]
            g = jnp.where(past, g, -jnp.inf)
            rank = jnp.zeros((nb, seq), F32)
            for m in range(nb):
                gm = g[m:m + 1, :]
                tie_first = (row_blk > m).astype(F32)
                rank = rank + jnp.where(gm > g, 1.0, jnp.where(gm == g, tie_first, 0.0))
            bias = jnp.where(((rank < MOBA_TOPK) & past) | (row_blk == key_blk), 0.0, MASKED).astype(BF16)
            for n in range(nb):
                qtb_ref[n, LANES:LANES + nb, h * blk:(h + 1) * blk] = bias[:, n * blk:(n + 1) * blk]

    _prepare_head_pair()

    key_pos = lax.broadcasted_iota(jnp.int32, (blk, width), 0)
    q_pos = lax.broadcasted_iota(jnp.int32, (blk, width), 1) & (blk - 1)
    causal = key_pos <= q_pos

    def rows_at(start, size):
        return pl.ds(pl.multiple_of(start, blk), size)

    def scores(c, qi, slot, cnt, own_j=None):
        s = jnp.dot(kx_ref[rows_at(c * chunk, cnt * blk), :], qtb_ref[qi], preferred_element_type=F32)
        m8 = None
        for j in range(cnt):
            sj = s[j * blk:(j + 1) * blk]
            if j == own_j:
                sj = jnp.where(causal, sj, MASKED)
            s_refs[slot][j * blk:(j + 1) * blk, :] = sj
            mj = jnp.max(sj.reshape(blk // 8, 8, width), axis=0)
            m8 = mj if m8 is None else jnp.maximum(m8, mj)
        m_old = mrun_ref[qi]
        m_new = jnp.maximum(m_old, jnp.max(m8, axis=0, keepdims=True))
        mrun_ref[qi] = m_new
        mnow_ref[slot] = m_new
        alpha_ref[slot] = jnp.exp2(m_old - m_new)

    def values(c, qi, slot, cnt):
        p = jnp.exp2(s_refs[slot][0:cnt * blk, :] - mnow_ref[slot, 0:1, :]).astype(BF16)
        pv = jnp.dot(vtc_ref[c, :, 0:cnt * blk], p, preferred_element_type=F32)
        acc_ref[qi] = acc_ref[qi] * alpha_ref[slot, 0:1, :] + pv

    def finish(qi):
        outs = []
        for h in range(nh):
            cols = slice(h * blk, (h + 1) * blk)
            outs.append(acc_ref[qi, h * hd:(h + 1) * hd, cols] / acc_ref[qi, LANES:LANES + 1, cols])
        attn = jnp.concatenate(outs, axis=0).T
        gate = g_ref[0, rows_at(qi * blk, blk), :].astype(F32)
        o_ref[0, rows_at(qi * blk, blk), :] = (attn * (gate * jax.nn.sigmoid(gate))).astype(o_ref.dtype)

    mrun_ref[...] = jnp.full(mrun_ref.shape, LOWEST, F32)
    acc_ref[...] = jnp.zeros(acc_ref.shape, F32)

    def key_chunk(c, carry):
        first = c * CHUNK_BLOCKS
        for j in range(CHUNK_BLOCKS):
            scores(c, first + j, j % 2, j + 1, own_j=j)
            values(c, first + j, j % 2, j + 1)
            finish(first + j)

        later = first + CHUNK_BLOCKS
        n_pairs = (nb - later) // 2

        @pl.when(n_pairs > 0)
        def _prime():
            scores(c, later, 0, CHUNK_BLOCKS)

        def query_pair(t, carry):
            qa = later + 2 * t
            scores(c, qa + 1, 1, CHUNK_BLOCKS)
            values(c, qa, 0, CHUNK_BLOCKS)
            scores(c, jnp.minimum(qa + 2, nb - 1), 0, CHUNK_BLOCKS)
            values(c, qa + 1, 1, CHUNK_BLOCKS)
            return carry

        lax.fori_loop(0, n_pairs, query_pair, 0)
        return carry

    lax.fori_loop(0, n_chunks, key_chunk, 0)


def _moba_attention(z3d):
    b, s, _ = z3d.shape
    nb = s // MOBA_BLOCK
    steps = ATTN_WIDTH // LANES
    width = HEADS_PER_STEP * MOBA_BLOCK
    vrows = LANES + ONES_ROWS
    kernel = functools.partial(_moba_kernel, nb=nb)
    full = lambda off: pl.BlockSpec((1, s, LANES), lambda bi, hp: (bi, 0, off * steps + hp))
    return pl.pallas_call(
        kernel,
        grid=(b, steps),
        in_specs=[full(0), full(1), full(2), full(3)],
        out_specs=pl.BlockSpec((1, s, LANES), lambda bi, hp: (bi, 0, hp)),
        out_shape=jax.ShapeDtypeStruct((b, s, ATTN_WIDTH), BF16),
        scratch_shapes=[pltpu.VMEM((s, 2 * LANES), BF16),
                        pltpu.VMEM((nb // CHUNK_BLOCKS, vrows, CHUNK_BLOCKS * MOBA_BLOCK), BF16),
                        pltpu.VMEM((nb, 2 * LANES, width), BF16),
                        pltpu.VMEM((CHUNK_BLOCKS * MOBA_BLOCK, width), F32),
                        pltpu.VMEM((CHUNK_BLOCKS * MOBA_BLOCK, width), F32),
                        pltpu.VMEM((nb, 8, width), F32),
                        pltpu.VMEM((2, 8, width), F32),
                        pltpu.VMEM((2, 8, width), F32),
                        pltpu.VMEM((nb, vrows, width), F32)],
        compiler_params=pltpu.CompilerParams(dimension_semantics=("arbitrary",) * 2,
                                             vmem_limit_bytes=VMEM_LIMIT),
        name="moba_attention",
    )(z3d, z3d, z3d, z3d)


def _retention_kernel(q_ref, k_ref, v_ref, g_ref, decay_ref, xi_ref, zeta_ref, gc_ref, gain_ref,
                      o_ref, state_ref):
    c = pl.program_id(0)
    d = RET_HEAD_DIM

    @pl.when(c == 0)
    def _reset_state():
        state_ref[...] = jnp.zeros_like(state_ref)

    for bi in range(q_ref.shape[0]):
        for h in range(RET_HEADS):
            cols = slice(h * d, (h + 1) * d)
            qb = q_ref[bi, :, cols]
            kb = k_ref[bi, :, cols]
            vb = v_ref[bi, :, cols]
            scores = lax.dot_general(qb, kb, (((1,), (1,)), ((), ())), preferred_element_type=F32)
            inner = (scores * decay_ref[h]).astype(BF16)
            state = state_ref[bi, h]
            out = (jnp.dot(inner, vb, preferred_element_type=F32)
                   + jnp.dot(qb, state.astype(BF16), preferred_element_type=F32) * xi_ref[h])
            vz = (vb.astype(F32) * zeta_ref[h]).astype(BF16)
            state_ref[bi, h] = state * gc_ref[h] + lax.dot_general(
                kb, vz, (((0,), (0,)), ((), ())), preferred_element_type=F32)

            mu = jnp.mean(out, axis=-1, keepdims=True)
            cen = out - mu
            var = jnp.mean(cen * cen, axis=-1, keepdims=True)
            normed = cen * lax.rsqrt(var + GN_EPS) * gain_ref[h]
            gate = g_ref[bi, :, cols].astype(F32)
            o_ref[bi, :, cols] = (normed * (gate * jax.nn.sigmoid(gate))).astype(o_ref.dtype)


def _retention_constants():
    d, c = RET_HEAD_DIM, RET_CHUNK
    log_g = jnp.log1p(-jnp.exp2(-5.0 - jnp.arange(RET_HEADS, dtype=F32)))
    idx = jnp.arange(c, dtype=F32)
    rel = idx[:, None] - idx[None, :]
    decay = jnp.where(rel >= 0, jnp.exp(jnp.maximum(rel, 0.0)[None] * log_g[:, None, None]), 0.0)
    xi = jnp.exp((idx + 1.0)[None] * log_g[:, None])
    zeta = jnp.exp((c - 1.0 - idx)[None] * log_g[:, None])
    g_chunk = jnp.exp(c * log_g)
    bcast = lambda t: jnp.broadcast_to(t[..., None], t.shape + (d,))
    return decay, bcast(xi), bcast(zeta), bcast(g_chunk[:, None])


def _retention(z3d, ret_gn_gain):
    b, s, _ = z3d.shape
    c, d = RET_CHUNK, RET_HEAD_DIM
    base = 4 * ATTN_WIDTH // RET_WIDTH
    decay, xi, zeta, gc = _retention_constants()
    gain = ret_gn_gain.astype(F32).reshape(RET_HEADS, 1, d)
    zspec = lambda off: pl.BlockSpec((b, c, RET_WIDTH), lambda ci: (0, ci, base + off))
    whole = lambda *shape: pl.BlockSpec(shape, lambda ci: (0,) * len(shape))
    return pl.pallas_call(
        _retention_kernel,
        grid=(s // c,),
        in_specs=[zspec(0), zspec(1), zspec(2), zspec(3),
                  whole(RET_HEADS, c, c), whole(RET_HEADS, c, d), whole(RET_HEADS, c, d),
                  whole(RET_HEADS, 1, d), whole(RET_HEADS, 1, d)],
        out_specs=pl.BlockSpec((b, c, RET_WIDTH), lambda ci: (0, ci, 0)),
        out_shape=jax.ShapeDtypeStruct((b, s, RET_WIDTH), BF16),
        scratch_shapes=[pltpu.VMEM((b, RET_HEADS, d, d), F32)],
        compiler_params=pltpu.CompilerParams(dimension_semantics=("arbitrary",),
                                             vmem_limit_bytes=VMEM_LIMIT),
        name="retention",
    )(z3d, z3d, z3d, z3d, decay, xi, zeta, gc, gain)


def _output_kernel(x_ref, a_ref, r_ref, p_ref, wa_ref, wr_ref, wg_ref, wp_ref, gain_ref, bias_ref, o_ref, *, sub):
    for r in range(o_ref.shape[0] // sub):
        rows = slice(r * sub, (r + 1) * sub)
        mix = (jnp.dot(a_ref[rows, :], wa_ref[...], preferred_element_type=F32)
               + jnp.dot(r_ref[rows, :], wr_ref[...], preferred_element_type=F32))
        u = DEEPNORM_ALPHA * x_ref[rows, :] + mix
        ple = jnp.dot(p_ref[rows, :].astype(BF16), wp_ref[...], preferred_element_type=F32)
        gate = jax.nn.sigmoid(jnp.dot(u.astype(BF16), wg_ref[...], preferred_element_type=F32))
        u = u + gate * ple
        mu = jnp.mean(u, axis=-1, keepdims=True)
        cen = u - mu
        var = jnp.mean(cen * cen, axis=-1, keepdims=True)
        o_ref[rows, :] = cen * lax.rsqrt(var + LN_EPS) * gain_ref[...] + bias_ref[...]


def _output_stage(x2d, attn2d, ret2d, p2d, w_out_bf16, w_gate_bf16, w_proj_bf16, ln_gain, ln_bias, *,
                  tm=1024, sub=256):
    m, dm = x2d.shape
    rows = lambda width: pl.BlockSpec((tm, width), lambda i: (i, 0))
    whole = lambda shape: pl.BlockSpec(shape, lambda i: (0, 0))
    return pl.pallas_call(
        functools.partial(_output_kernel, sub=sub),
        grid=(m // tm,),
        in_specs=[rows(dm), rows(ATTN_WIDTH), rows(RET_WIDTH), rows(D_PLE),
                  pl.BlockSpec((ATTN_WIDTH, dm), lambda i: (0, 0)),
                  pl.BlockSpec((RET_WIDTH, dm), lambda i: (1, 0)),
                  whole((dm, dm)), whole((D_PLE, dm)), whole((1, dm)), whole((1, dm))],
        out_specs=rows(dm),
        out_shape=jax.ShapeDtypeStruct((m, dm), F32),
        compiler_params=pltpu.CompilerParams(dimension_semantics=("arbitrary",),
                                             vmem_limit_bytes=VMEM_LIMIT),
        name="output_stage",
    )(x2d, attn2d, ret2d, p2d, w_out_bf16, w_out_bf16, w_gate_bf16, w_proj_bf16,
      ln_gain.astype(F32).reshape(1, dm), ln_bias.astype(F32).reshape(1, dm))


def kernel(x, p, w_in, w_out, ret_gn_gain, w_ple_gate, w_ple_proj, ln_gain, ln_bias):
    b, s, dm = x.shape
    h = x
    for i in range(DEPTH):
        h2d = h.reshape(b * s, dm)
        z = _input_projection(h2d, w_in[i].astype(BF16), s)
        z3d = z.reshape(b, s, IN_WIDTH)
        attn = _moba_attention(z3d)
        ret = _retention(z3d, ret_gn_gain[i])
        out = _output_stage(h2d, attn.reshape(b * s, ATTN_WIDTH), ret.reshape(b * s, RET_WIDTH),
                            p[i].reshape(b * s, D_PLE), w_out[i].astype(BF16), w_ple_gate[i].astype(BF16),
                            w_ple_proj[i].astype(BF16), ln_gain[i], ln_bias[i])
        h = out.reshape(b, s, dm)
    return h
```

```python
import functools

import jax
import jax.numpy as jnp
from jax import lax
from jax.experimental import pallas as pl
from jax.experimental.pallas import tpu as pltpu

F32 = jnp.float32
BF16 = jnp.bfloat16

D_MODEL = 1024
D_PLE = 256
ATTN_HEADS = 8
ATTN_HEAD_DIM = 64
ATTN_WIDTH = ATTN_HEADS * ATTN_HEAD_DIM
RET_HEADS = 4
RET_HEAD_DIM = 128
RET_WIDTH = RET_HEADS * RET_HEAD_DIM
IN_WIDTH = 4 * ATTN_WIDTH + 4 * RET_WIDTH
MOBA_BLOCK = 256
MOBA_TOPK = 3
ROPE_BASE = 10000.0
DEPTH = 1
DEEPNORM_ALPHA = (2.0 * DEPTH) ** 0.25
LN_EPS = 1e-5
GN_EPS = 1e-6

LANES = 128
HEADS_PER_STEP = LANES // ATTN_HEAD_DIM
MASKED = -1e30
LOWEST = -3e38
LOG2_E = 1.4426950408889634
ONES_ROWS = 16
CHUNK_BLOCKS = 4
RET_CHUNK = 256
VMEM_LIMIT = 48 * 1024 * 1024


def _proj_kernel(x_ref, w_ref, cos_ref, sin_ref, o_ref, *, n_chunk):
    xb = x_ref[...].astype(BF16)
    d = RET_HEAD_DIM
    rot_lo = 4 * ATTN_WIDTH
    rot_mid = rot_lo + RET_WIDTH
    rot_hi = rot_mid + RET_WIDTH
    assert rot_lo % n_chunk == 0 and rot_hi % n_chunk == 0 and n_chunk % d == 0
    for lo in range(0, o_ref.shape[1], n_chunk):
        acc = jnp.dot(xb, w_ref[:, lo:lo + n_chunk], preferred_element_type=F32)
        if rot_lo <= lo < rot_hi:
            cos = cos_ref[...]
            sin = sin_ref[...]
            for col in range(lo, lo + n_chunk, d):
                head = acc[:, col - lo:col - lo + d]
                rotated = head * cos + pltpu.roll(head, d // 2, 1) * sin
                if col >= rot_mid:
                    rotated = rotated * (d ** -0.5)
                o_ref[:, col:col + d] = rotated.astype(o_ref.dtype)
        else:
            o_ref[:, lo:lo + n_chunk] = acc.astype(o_ref.dtype)


def _rotary_tables(s):
    half = RET_HEAD_DIM // 2
    freqs = ROPE_BASE ** (-jnp.arange(half, dtype=F32) / half)
    ang = jnp.arange(s).astype(F32)[:, None] * freqs[None, :]
    cos, sin = jnp.cos(ang), jnp.sin(ang)
    return jnp.concatenate([cos, cos], axis=-1), jnp.concatenate([-sin, sin], axis=-1)


def _input_projection(x2d, w_in_bf16, seq, *, tm=512, n_chunk=1024):
    m, k = x2d.shape
    n = w_in_bf16.shape[1]
    cos, sin = _rotary_tables(seq)
    pos = pl.BlockSpec((tm, RET_HEAD_DIM), lambda i: (i % (seq // tm), 0))
    return pl.pallas_call(
        functools.partial(_proj_kernel, n_chunk=n_chunk),
        grid=(m // tm,),
        in_specs=[pl.BlockSpec((tm, k), lambda i: (i, 0)),
                  pl.BlockSpec((k, n), lambda i: (0, 0)),
                  pos, pos],
        out_specs=pl.BlockSpec((tm, n), lambda i: (i, 0)),
        out_shape=jax.ShapeDtypeStruct((m, n), BF16),
        compiler_params=pltpu.CompilerParams(dimension_semantics=("arbitrary",),
                                             vmem_limit_bytes=VMEM_LIMIT),
        name="input_projection",
    )(x2d, w_in_bf16, cos, sin)


def _moba_kernel(q_ref, k_ref, v_ref, g_ref, o_ref,
                 kx_ref, vtc_ref, qtb_ref, s0_ref, s1_ref, mrun_ref, mnow_ref, alpha_ref, acc_ref, *, nb):
    blk = MOBA_BLOCK
    hd = ATTN_HEAD_DIM
    nh = HEADS_PER_STEP
    width = nh * blk
    vrows = LANES + ONES_ROWS
    seq = nb * blk
    chunk = CHUNK_BLOCKS * blk
    n_chunks = nb // CHUNK_BLOCKS
    s_refs = (s0_ref, s1_ref)

    def _prepare_head_pair():
        ones = jnp.ones((ONES_ROWS, blk), BF16)
        for n in range(nb):
            vt = v_ref[0, n * blk:(n + 1) * blk, :].astype(F32).T.astype(BF16)
            c, j = divmod(n, CHUNK_BLOCKS)
            vtc_ref[c, 0:LANES, j * blk:(j + 1) * blk] = vt
            vtc_ref[c, LANES:vrows, j * blk:(j + 1) * blk] = ones
        kx_ref[:, 0:LANES] = k_ref[0]
        key_row_blk = lax.broadcasted_iota(jnp.int32, (seq, LANES), 0) // blk
        kx_ref[:, LANES:2 * LANES] = jnp.where(
            key_row_blk == lax.broadcasted_iota(jnp.int32, (seq, LANES), 1), 1.0, 0.0).astype(BF16)
        key_blk = lax.broadcasted_iota(jnp.int32, (nb, seq), 1) // blk
        row_blk = lax.broadcasted_iota(jnp.int32, (nb, seq), 0)
        averager = jnp.where(key_blk == row_blk, 1.0 / blk, 0.0).astype(BF16)
        kmean = jnp.dot(averager, k_ref[0], preferred_element_type=F32)
        hi = kmean.astype(BF16)
        lo = (kmean - hi.astype(F32)).astype(BF16)
        lane = lax.broadcasted_iota(jnp.int32, (nb, LANES), 1)
        parts = []
        for h in range(nh):
            in_head = (lane >= h * hd) & (lane < (h + 1) * hd)
            parts += [jnp.where(in_head, hi, jnp.zeros_like(hi)), jnp.where(in_head, lo, jnp.zeros_like(lo))]
        kstack = jnp.concatenate(parts, axis=0)
        feat = lax.broadcasted_iota(jnp.int32, (LANES, blk), 0)
        qts = []
        for n in range(nb):
            qt = (q_ref[0, n * blk:(n + 1) * blk, :].astype(F32) * (hd ** -0.5 * LOG2_E)).T.astype(BF16)
            for h in range(nh):
                in_head = (feat >= h * hd) & (feat < (h + 1) * hd)
                qtb_ref[n, 0:LANES, h * blk:(h + 1) * blk] = jnp.where(in_head, qt, jnp.zeros_like(qt))
            qtb_ref[n, LANES + nb:2 * LANES, :] = jnp.zeros((LANES - nb, width), BF16)
            qts.append(qt)
        gates = jnp.dot(kstack, jnp.concatenate(qts, axis=1), preferred_element_type=F32)
        past = row_blk < key_blk
        for h in range(nh):
            g = gates[(2 * h) * nb:(2 * h + 1) * nb] + gates[(2 * h + 1) * nb:(2 * h + 2) * nb]
            g = jnp.where(past, g, -jnp.inf)
            rank = jnp.zeros((nb, seq), F32)
            for m in range(nb):
                gm = g[m:m + 1, :]
                tie_first = (row_blk > m).astype(F32)
                rank = rank + jnp.where(gm > g, 1.0, jnp.where(gm == g, tie_first, 0.0))
            bias = jnp.where(((rank < MOBA_TOPK) & past) | (row_blk == key_blk), 0.0, MASKED).astype(BF16)
            for n in range(nb):
                qtb_ref[n, LANES:LANES + nb, h * blk:(h + 1) * blk] = bias[:, n * blk:(n + 1) * blk]

    _prepare_head_pair()

    key_pos = lax.broadcasted_iota(jnp.int32, (blk, width), 0)
    q_pos = lax.broadcasted_iota(jnp.int32, (blk, width), 1) & (blk - 1)
    causal = key_pos <= q_pos

    def rows_at(start, size):
        if isinstance(start, int):
            return pl.ds(start, size)
        return pl.ds(pl.multiple_of(start, blk), size)

    def scores(c, qi, slot, cnt, own_j=None):
        s = jnp.dot(kx_ref[rows_at(c * chunk, cnt * blk), :], qtb_ref[qi], preferred_element_type=F32)
        m8 = None
        for j in range(cnt):
            sj = s[j * blk:(j + 1) * blk]
            if j == own_j:
                sj = jnp.where(causal, sj, MASKED)
            s_refs[slot][j * blk:(j + 1) * blk, :] = sj
            mj = jnp.max(sj.reshape(blk // 8, 8, width), axis=0)
            m8 = mj if m8 is None else jnp.maximum(m8, mj)
        m_old = mrun_ref[qi]
        m_new = jnp.maximum(m_old, jnp.max(m8, axis=0, keepdims=True))
        mrun_ref[qi] = m_new
        mnow_ref[slot] = m_new
        alpha_ref[slot] = jnp.exp2(m_old - m_new)

    def values(c, qi, slot, cnt):
        p = jnp.exp2(s_refs[slot][0:cnt * blk, :] - mnow_ref[slot, 0:1, :]).astype(BF16)
        pv = jnp.dot(vtc_ref[c, :, 0:cnt * blk], p, preferred_element_type=F32)
        acc_ref[qi] = acc_ref[qi] * alpha_ref[slot, 0:1, :] + pv

    def finish(qi):
        outs = []
        for h in range(nh):
            cols = slice(h * blk, (h + 1) * blk)
            outs.append(acc_ref[qi, h * hd:(h + 1) * hd, cols] / acc_ref[qi, LANES:LANES + 1, cols])
        attn = jnp.concatenate(outs, axis=0).T
        gate = g_ref[0, rows_at(qi * blk, blk), :].astype(F32)
        o_ref[0, rows_at(qi * blk, blk), :] = (attn * (gate * jax.nn.sigmoid(gate))).astype(o_ref.dtype)

    mrun_ref[...] = jnp.full(mrun_ref.shape, LOWEST, F32)
    acc_ref[...] = jnp.zeros(acc_ref.shape, F32)

    assert CHUNK_BLOCKS % 2 == 0 and nb % CHUNK_BLOCKS == 0
    for c in range(n_chunks):
        first = c * CHUNK_BLOCKS
        later = first + CHUNK_BLOCKS
        n_pairs = (nb - later) // 2
        scores(c, first, 0, 1, own_j=0)
        for j in range(CHUNK_BLOCKS):
            if j + 1 < CHUNK_BLOCKS:
                scores(c, first + j + 1, (j + 1) % 2, j + 2, own_j=j + 1)
            elif n_pairs > 0:
                scores(c, later, 0, CHUNK_BLOCKS)
            values(c, first + j, j % 2, j + 1)
            finish(first + j)
        if n_pairs == 0:
            continue

        def query_pair(t, carry, c=c, later=later):
            qa = later + 2 * t
            scores(c, qa + 1, 1, CHUNK_BLOCKS)
            values(c, qa, 0, CHUNK_BLOCKS)
            scores(c, qa + 2, 0, CHUNK_BLOCKS)
            values(c, qa + 1, 1, CHUNK_BLOCKS)
            return carry

        lax.fori_loop(0, n_pairs - 1, query_pair, 0)
        scores(c, nb - 1, 1, CHUNK_BLOCKS)
        values(c, nb - 2, 0, CHUNK_BLOCKS)
        values(c, nb - 1, 1, CHUNK_BLOCKS)


def _moba_attention(z3d):
    b, s, _ = z3d.shape
    nb = s // MOBA_BLOCK
    steps = ATTN_WIDTH // LANES
    width = HEADS_PER_STEP * MOBA_BLOCK
    vrows = LANES + ONES_ROWS
    kernel = functools.partial(_moba_kernel, nb=nb)
    full = lambda off: pl.BlockSpec((1, s, LANES), lambda bi, hp: (bi, 0, off * steps + hp))
    return pl.pallas_call(
        kernel,
        grid=(b, steps),
        in_specs=[full(0), full(1), full(2), full(3)],
        out_specs=pl.BlockSpec((1, s, LANES), lambda bi, hp: (bi, 0, hp)),
        out_shape=jax.ShapeDtypeStruct((b, s, ATTN_WIDTH), BF16),
        scratch_shapes=[pltpu.VMEM((s, 2 * LANES), BF16),
                        pltpu.VMEM((nb // CHUNK_BLOCKS, vrows, CHUNK_BLOCKS * MOBA_BLOCK), BF16),
                        pltpu.VMEM((nb, 2 * LANES, width), BF16),
                        pltpu.VMEM((CHUNK_BLOCKS * MOBA_BLOCK, width), F32),
                        pltpu.VMEM((CHUNK_BLOCKS * MOBA_BLOCK, width), F32),
                        pltpu.VMEM((nb, 8, width), F32),
                        pltpu.VMEM((2, 8, width), F32),
                        pltpu.VMEM((2, 8, width), F32),
                        pltpu.VMEM((nb, vrows, width), F32)],
        compiler_params=pltpu.CompilerParams(dimension_semantics=("arbitrary",) * 2,
                                             vmem_limit_bytes=VMEM_LIMIT),
        name="moba_attention",
    )(z3d, z3d, z3d, z3d)


def _retention_kernel(q_ref, k_ref, v_ref, g_ref, decay_ref, xi_ref, zeta_ref, gc_ref, gain_ref,
                      o_ref, state_ref):
    c = pl.program_id(0)
    d = RET_HEAD_DIM

    @pl.when(c == 0)
    def _reset_state():
        state_ref[...] = jnp.zeros_like(state_ref)

    for bi in range(q_ref.shape[0]):
        for h in range(RET_HEADS):
            cols = slice(h * d, (h + 1) * d)
            qb = q_ref[bi, :, cols]
            kb = k_ref[bi, :, cols]
            vb = v_ref[bi, :, cols]
            scores = lax.dot_general(qb, kb, (((1,), (1,)), ((), ())), preferred_element_type=F32)
            inner = (scores * decay_ref[h]).astype(BF16)
            state = state_ref[bi, h]
            out = (jnp.dot(inner, vb, preferred_element_type=F32)
                   + jnp.dot(qb, state.astype(BF16), preferred_element_type=F32) * xi_ref[h])
            vz = (vb.astype(F32) * zeta_ref[h]).astype(BF16)
            state_ref[bi, h] = state * gc_ref[h] + lax.dot_general(
                kb, vz, (((0,), (0,)), ((), ())), preferred_element_type=F32)

            mu = jnp.mean(out, axis=-1, keepdims=True)
            cen = out - mu
            var = jnp.mean(cen * cen, axis=-1, keepdims=True)
            normed = cen * lax.rsqrt(var + GN_EPS) * gain_ref[h]
            gate = g_ref[bi, :, cols].astype(F32)
            o_ref[bi, :, cols] = (normed * (gate * jax.nn.sigmoid(gate))).astype(o_ref.dtype)


def _retention_constants():
    d, c = RET_HEAD_DIM, RET_CHUNK
    log_g = jnp.log1p(-jnp.exp2(-5.0 - jnp.arange(RET_HEADS, dtype=F32)))
    idx = jnp.arange(c, dtype=F32)
    rel = idx[:, None] - idx[None, :]
    decay = jnp.where(rel >= 0, jnp.exp(jnp.maximum(rel, 0.0)[None] * log_g[:, None, None]), 0.0)
    xi = jnp.exp((idx + 1.0)[None] * log_g[:, None])
    zeta = jnp.exp((c - 1.0 - idx)[None] * log_g[:, None])
    g_chunk = jnp.exp(c * log_g)
    bcast = lambda t: jnp.broadcast_to(t[..., None], t.shape + (d,))
    return decay, bcast(xi), bcast(zeta), bcast(g_chunk[:, None])


def _retention(z3d, ret_gn_gain):
    b, s, _ = z3d.shape
    c, d = RET_CHUNK, RET_HEAD_DIM
    base = 4 * ATTN_WIDTH // RET_WIDTH
    decay, xi, zeta, gc = _retention_constants()
    gain = ret_gn_gain.astype(F32).reshape(RET_HEADS, 1, d)
    zspec = lambda off: pl.BlockSpec((b, c, RET_WIDTH), lambda ci: (0, ci, base + off))
    whole = lambda *shape: pl.BlockSpec(shape, lambda ci: (0,) * len(shape))
    return pl.pallas_call(
        _retention_kernel,
        grid=(s // c,),
        in_specs=[zspec(0), zspec(1), zspec(2), zspec(3),
                  whole(RET_HEADS, c, c), whole(RET_HEADS, c, d), whole(RET_HEADS, c, d),
                  whole(RET_HEADS, 1, d), whole(RET_HEADS, 1, d)],
        out_specs=pl.BlockSpec((b, c, RET_WIDTH), lambda ci: (0, ci, 0)),
        out_shape=jax.ShapeDtypeStruct((b, s, RET_WIDTH), BF16),
        scratch_shapes=[pltpu.VMEM((b, RET_HEADS, d, d), F32)],
        compiler_params=pltpu.CompilerParams(dimension_semantics=("arbitrary",),
                                             vmem_limit_bytes=VMEM_LIMIT),
        name="retention",
    )(z3d, z3d, z3d, z3d, decay, xi, zeta, gc, gain)


def _output_kernel(x_ref, a_ref, r_ref, p_ref, wa_ref, wr_ref, wg_ref, wp_ref, gain_ref, bias_ref, o_ref, *, sub):
    for r in range(o_ref.shape[0] // sub):
        rows = slice(r * sub, (r + 1) * sub)
        mix = (jnp.dot(a_ref[rows, :], wa_ref[...], preferred_element_type=F32)
               + jnp.dot(r_ref[rows, :], wr_ref[...], preferred_element_type=F32))
        u = DEEPNORM_ALPHA * x_ref[rows, :] + mix
        ple = jnp.dot(p_ref[rows, :].astype(BF16), wp_ref[...], preferred_element_type=F32)
        gate = jax.nn.sigmoid(jnp.dot(u.astype(BF16), wg_ref[...], preferred_element_type=F32))
        u = u + gate * ple
        mu = jnp.mean(u, axis=-1, keepdims=True)
        cen = u - mu
        var = jnp.mean(cen * cen, axis=-1, keepdims=True)
        o_ref[rows, :] = cen * lax.rsqrt(var + LN_EPS) * gain_ref[...] + bias_ref[...]


def _output_stage(x2d, attn2d, ret2d, p2d, w_out_bf16, w_gate_bf16, w_proj_bf16, ln_gain, ln_bias, *,
                  tm=1024, sub=256):
    m, dm = x2d.shape
    rows = lambda width: pl.BlockSpec((tm, width), lambda i: (i, 0))
    whole = lambda shape: pl.BlockSpec(shape, lambda i: (0, 0))
    return pl.pallas_call(
        functools.partial(_output_kernel, sub=sub),
        grid=(m // tm,),
        in_specs=[rows(dm), rows(ATTN_WIDTH), rows(RET_WIDTH), rows(D_PLE),
                  pl.BlockSpec((ATTN_WIDTH, dm), lambda i: (0, 0)),
                  pl.BlockSpec((RET_WIDTH, dm), lambda i: (1, 0)),
                  whole((dm, dm)), whole((D_PLE, dm)), whole((1, dm)), whole((1, dm))],
        out_specs=rows(dm),
        out_shape=jax.ShapeDtypeStruct((m, dm), F32),
        compiler_params=pltpu.CompilerParams(dimension_semantics=("arbitrary",),
                                             vmem_limit_bytes=VMEM_LIMIT),
        name="output_stage",
    )(x2d, attn2d, ret2d, p2d, w_out_bf16, w_out_bf16, w_gate_bf16, w_proj_bf16,
      ln_gain.astype(F32).reshape(1, dm), ln_bias.astype(F32).reshape(1, dm))


def kernel(x, p, w_in, w_out, ret_gn_gain, w_ple_gate, w_ple_proj, ln_gain, ln_bias):
    b, s, dm = x.shape
    h = x
    for i in range(DEPTH):
        h2d = h.reshape(b * s, dm)
        z = _input_projection(h2d, w_in[i].astype(BF16), s)
        z3d = z.reshape(b, s, IN_WIDTH)
        attn = _moba_attention(z3d)
        ret = _retention(z3d, ret_gn_gain[i])
        out = _output_stage(h2d, attn.reshape(b * s, ATTN_WIDTH), ret.reshape(b * s, RET_WIDTH),
                            p[i].reshape(b * s, D_PLE), w_out[i].astype(BF16), w_ple_gate[i].astype(BF16),
                            w_ple_proj[i].astype(BF16), ln_gain[i], ln_bias[i])
        h = out.reshape(b, s, dm)
    return h
```

```python
import functools

import jax
import jax.numpy as jnp
from jax import lax
from jax.experimental import pallas as pl
from jax.experimental.pallas import tpu as pltpu

F32 = jnp.float32
BF16 = jnp.bfloat16

D_MODEL = 1024
D_PLE = 256
ATTN_HEADS = 8
ATTN_HEAD_DIM = 64
ATTN_WIDTH = ATTN_HEADS * ATTN_HEAD_DIM
RET_HEADS = 4
RET_HEAD_DIM = 128
RET_WIDTH = RET_HEADS * RET_HEAD_DIM
IN_WIDTH = 4 * ATTN_WIDTH + 4 * RET_WIDTH
MOBA_BLOCK = 256
MOBA_TOPK = 3
ROPE_BASE = 10000.0
DEPTH = 1
DEEPNORM_ALPHA = (2.0 * DEPTH) ** 0.25
LN_EPS = 1e-5
GN_EPS = 1e-6

LANES = 128
HEADS_PER_STEP = LANES // ATTN_HEAD_DIM
MASKED = -1e30
LOG2_E = 1.4426950408889634
ONES_ROWS = 16
CHUNK_BLOCKS = 4
RET_CHUNK = 256
VMEM_LIMIT = 48 * 1024 * 1024


def _proj_kernel(x_ref, w32_ref, cos_ref, sin_ref, o_ref, w_ref, *, n_chunk):
    @pl.when(pl.program_id(0) == 0)
    def _cast_weights_once():
        for lo in range(0, w_ref.shape[1], n_chunk):
            w_ref[:, lo:lo + n_chunk] = w32_ref[:, lo:lo + n_chunk].astype(BF16)

    xb = x_ref[...].astype(BF16)
    d = RET_HEAD_DIM
    rot_lo = 4 * ATTN_WIDTH
    rot_mid = rot_lo + RET_WIDTH
    rot_hi = rot_mid + RET_WIDTH
    assert rot_lo % n_chunk == 0 and rot_hi % n_chunk == 0 and n_chunk % d == 0
    for lo in range(0, o_ref.shape[1], n_chunk):
        acc = jnp.dot(xb, w_ref[:, lo:lo + n_chunk], preferred_element_type=F32)
        if rot_lo <= lo < rot_hi:
            cos = cos_ref[...]
            sin = sin_ref[...]
            for col in range(lo, lo + n_chunk, d):
                head = acc[:, col - lo:col - lo + d]
                rotated = head * cos + pltpu.roll(head, d // 2, 1) * sin
                if col >= rot_mid:
                    rotated = rotated * (d ** -0.5)
                o_ref[:, col:col + d] = rotated.astype(o_ref.dtype)
        else:
            o_ref[:, lo:lo + n_chunk] = acc.astype(o_ref.dtype)


def _rotary_tables(s):
    half = RET_HEAD_DIM // 2
    freqs = ROPE_BASE ** (-jnp.arange(half, dtype=F32) / half)
    ang = jnp.arange(s).astype(F32)[:, None] * freqs[None, :]
    cos, sin = jnp.cos(ang), jnp.sin(ang)
    return jnp.concatenate([cos, cos], axis=-1), jnp.concatenate([-sin, sin], axis=-1)


def _resident(shape):
    return pl.BlockSpec(shape, lambda *_: (0,) * len(shape), pipeline_mode=pl.Buffered(1))


def _input_projection(x2d, w_in, seq, *, tm=512, n_chunk=1024):
    m, k = x2d.shape
    n = w_in.shape[1]
    cos, sin = _rotary_tables(seq)
    pos = pl.BlockSpec((tm, RET_HEAD_DIM), lambda i: (i % (seq // tm), 0))
    return pl.pallas_call(
        functools.partial(_proj_kernel, n_chunk=n_chunk),
        grid=(m // tm,),
        in_specs=[pl.BlockSpec((tm, k), lambda i: (i, 0)), _resident((k, n)), pos, pos],
        out_specs=pl.BlockSpec((tm, n), lambda i: (i, 0)),
        out_shape=jax.ShapeDtypeStruct((m, n), BF16),
        scratch_shapes=[pltpu.VMEM((k, n), BF16)],
        compiler_params=pltpu.CompilerParams(dimension_semantics=("arbitrary",),
                                             vmem_limit_bytes=VMEM_LIMIT),
        name="input_projection",
    )(x2d, w_in, cos, sin)


def _moba_kernel(q_ref, k_ref, v_ref, g_ref, onehot_ref, o_ref,
                 vtc_ref, qtb_ref, s0_ref, s1_ref, mrun_ref, mnow_ref, alpha_ref, acc_ref, *, nb):
    blk = MOBA_BLOCK
    hd = ATTN_HEAD_DIM
    nh = HEADS_PER_STEP
    width = nh * blk
    vrows = LANES + ONES_ROWS
    seq = nb * blk
    chunk = CHUNK_BLOCKS * blk
    n_chunks = nb // CHUNK_BLOCKS
    s_refs = (s0_ref, s1_ref)

    def _prepare_head_pair():
        ones = jnp.ones((ONES_ROWS, blk), BF16)
        for n in range(nb):
            vt = v_ref[0, n * blk:(n + 1) * blk, :].astype(F32).T.astype(BF16)
            c, j = divmod(n, CHUNK_BLOCKS)
            vtc_ref[c, 0:LANES, j * blk:(j + 1) * blk] = vt
            vtc_ref[c, LANES:vrows, j * blk:(j + 1) * blk] = ones
        key_blk = lax.broadcasted_iota(jnp.int32, (nb, seq), 1) // blk
        row_blk = lax.broadcasted_iota(jnp.int32, (nb, seq), 0)
        averager = jnp.where(key_blk == row_blk, 1.0 / blk, 0.0).astype(BF16)
        kmean = jnp.dot(averager, k_ref[0], preferred_element_type=F32)
        hi = kmean.astype(BF16)
        lo = (kmean - hi.astype(F32)).astype(BF16)
        lane = lax.broadcasted_iota(jnp.int32, (nb, LANES), 1)
        parts = []
        for h in range(nh):
            in_head = (lane >= h * hd) & (lane < (h + 1) * hd)
            parts += [jnp.where(in_head, hi, jnp.zeros_like(hi)), jnp.where(in_head, lo, jnp.zeros_like(lo))]
        kstack = jnp.concatenate(parts, axis=0)
        feat = lax.broadcasted_iota(jnp.int32, (LANES, blk), 0)
        qts = []
        for n in range(nb):
            qt = (q_ref[0, n * blk:(n + 1) * blk, :].astype(F32) * (hd ** -0.5 * LOG2_E)).T.astype(BF16)
            for h in range(nh):
                in_head = (feat >= h * hd) & (feat < (h + 1) * hd)
                qtb_ref[n, 0:LANES, h * blk:(h + 1) * blk] = jnp.where(in_head, qt, jnp.zeros_like(qt))
            qtb_ref[n, LANES + nb:2 * LANES, :] = jnp.zeros((LANES - nb, width), BF16)
            qts.append(qt)
        gates = jnp.dot(kstack, jnp.concatenate(qts, axis=1), preferred_element_type=F32)
        past = row_blk < key_blk
        row_f = row_blk.astype(F32)
        own_f = jnp.where(row_blk == key_blk, 1.0, 0.0)
        for h in range(nh):
            g = gates[(2 * h) * nb:(2 * h + 1) * nb] + gates[(2 * h + 1) * nb:(2 * h + 2) * nb]
            g = jnp.where(past, g, -jnp.inf)
            picked = jnp.zeros((nb, seq), F32)
            for _ in range(MOBA_TOPK):
                top = jnp.max(g, axis=0, keepdims=True)
                first = jnp.min(jnp.where(g == top, row_f, float(nb)), axis=0, keepdims=True)
                pick = row_f == first
                picked = jnp.where(pick, 1.0, picked)
                g = jnp.where(pick, -jnp.inf, g)
            bias = jnp.where(jnp.where(past, picked, own_f) > 0.5, 0.0, MASKED).astype(BF16)
            for n in range(nb):
                qtb_ref[n, LANES:LANES + nb, h * blk:(h + 1) * blk] = bias[:, n * blk:(n + 1) * blk]

    _prepare_head_pair()

    key_pos = lax.broadcasted_iota(jnp.int32, (blk, width), 0)
    q_pos = lax.broadcasted_iota(jnp.int32, (blk, width), 1) & (blk - 1)
    causal = key_pos <= q_pos

    def rows_at(start, size):
        if isinstance(start, int):
            return pl.ds(start, size)
        return pl.ds(pl.multiple_of(start, blk), size)

    def scores(c, qi, slot, cnt, own_j=None):
        key_rows = rows_at(c * chunk, cnt * blk)
        keys_ext = jnp.concatenate([k_ref[0, key_rows, :], onehot_ref[key_rows, :]], axis=1)
        s = jnp.dot(keys_ext, qtb_ref[qi], preferred_element_type=F32)
        m8 = None
        for j in range(cnt):
            sj = s[j * blk:(j + 1) * blk]
            if j == own_j:
                sj = jnp.where(causal, sj, MASKED)
            s_refs[slot][j * blk:(j + 1) * blk, :] = sj
            mj = jnp.max(sj.reshape(blk // 8, 8, width), axis=0)
            m8 = mj if m8 is None else jnp.maximum(m8, mj)
        m_new = jnp.broadcast_to(jnp.max(m8, axis=0, keepdims=True), (8, width))
        if c > 0:
            m_old = mrun_ref[qi]
            m_new = jnp.maximum(m_old, m_new)
            alpha_ref[slot] = jnp.exp2(m_old - m_new)
        mrun_ref[qi] = m_new
        mnow_ref[slot] = m_new

    def values(c, qi, slot, cnt):
        p = jnp.exp2(s_refs[slot][0:cnt * blk, :] - mnow_ref[slot, 0:1, :]).astype(BF16)
        pv = jnp.dot(vtc_ref[c, :, 0:cnt * blk], p, preferred_element_type=F32)
        if c > 0:
            pv = acc_ref[qi] * alpha_ref[slot, 0:1, :] + pv
        acc_ref[qi] = pv

    def finish(qi):
        outs = []
        for h in range(nh):
            cols = slice(h * blk, (h + 1) * blk)
            outs.append(acc_ref[qi, h * hd:(h + 1) * hd, cols] / acc_ref[qi, LANES:LANES + 1, cols])
        attn = jnp.concatenate(outs, axis=0).T
        gate = g_ref[0, rows_at(qi * blk, blk), :].astype(F32)
        o_ref[0, rows_at(qi * blk, blk), :] = (attn * (gate * jax.nn.sigmoid(gate))).astype(o_ref.dtype)

    assert CHUNK_BLOCKS % 2 == 0 and nb % CHUNK_BLOCKS == 0
    for c in range(n_chunks):
        first = c * CHUNK_BLOCKS
        later = first + CHUNK_BLOCKS
        n_pairs = (nb - later) // 2
        scores(c, first, 0, 1, own_j=0)
        for j in range(CHUNK_BLOCKS):
            if j + 1 < CHUNK_BLOCKS:
                scores(c, first + j + 1, (j + 1) % 2, j + 2, own_j=j + 1)
            elif n_pairs > 0:
                scores(c, later, 0, CHUNK_BLOCKS)
            values(c, first + j, j % 2, j + 1)
            finish(first + j)
        if n_pairs == 0:
            continue

        def query_pair(t, carry, c=c, later=later):
            qa = later + 2 * t
            scores(c, qa + 1, 1, CHUNK_BLOCKS)
            values(c, qa, 0, CHUNK_BLOCKS)
            scores(c, qa + 2, 0, CHUNK_BLOCKS)
            values(c, qa + 1, 1, CHUNK_BLOCKS)
            return carry

        lax.fori_loop(0, n_pairs - 1, query_pair, 0)
        scores(c, nb - 1, 1, CHUNK_BLOCKS)
        values(c, nb - 2, 0, CHUNK_BLOCKS)
        values(c, nb - 1, 1, CHUNK_BLOCKS)


def _moba_attention(z3d):
    b, s, _ = z3d.shape
    nb = s // MOBA_BLOCK
    steps = ATTN_WIDTH // LANES
    width = HEADS_PER_STEP * MOBA_BLOCK
    vrows = LANES + ONES_ROWS
    kernel = functools.partial(_moba_kernel, nb=nb)
    full = lambda off: pl.BlockSpec((1, s, LANES), lambda bi, hp: (bi, 0, off * steps + hp))
    onehot = (jnp.arange(s)[:, None] // MOBA_BLOCK == jnp.arange(LANES)[None, :]).astype(BF16)
    return pl.pallas_call(
        kernel,
        grid=(b, steps),
        in_specs=[full(0), full(1), full(2), full(3), _resident((s, LANES))],
        out_specs=pl.BlockSpec((1, s, LANES), lambda bi, hp: (bi, 0, hp)),
        out_shape=jax.ShapeDtypeStruct((b, s, ATTN_WIDTH), BF16),
        scratch_shapes=[pltpu.VMEM((nb // CHUNK_BLOCKS, vrows, CHUNK_BLOCKS * MOBA_BLOCK), BF16),
                        pltpu.VMEM((nb, 2 * LANES, width), BF16),
                        pltpu.VMEM((CHUNK_BLOCKS * MOBA_BLOCK, width), F32),
                        pltpu.VMEM((CHUNK_BLOCKS * MOBA_BLOCK, width), F32),
                        pltpu.VMEM((nb, 8, width), F32),
                        pltpu.VMEM((2, 8, width), F32),
                        pltpu.VMEM((2, 8, width), F32),
                        pltpu.VMEM((nb, vrows, width), F32)],
        compiler_params=pltpu.CompilerParams(dimension_semantics=("arbitrary",) * 2,
                                             vmem_limit_bytes=VMEM_LIMIT),
        name="moba_attention",
    )(z3d, z3d, z3d, z3d, onehot)


def _retention_kernel(q_ref, k_ref, v_ref, g_ref, decay_ref, xi_ref, zeta_ref, gc_ref, gain_ref,
                      o_ref, state_ref):
    c = pl.program_id(0)
    d = RET_HEAD_DIM

    @pl.when(c == 0)
    def _reset_state():
        state_ref[...] = jnp.zeros_like(state_ref)

    for bi in range(q_ref.shape[0]):
        for h in range(RET_HEADS):
            cols = slice(h * d, (h + 1) * d)
            qb = q_ref[bi, :, cols]
            kb = k_ref[bi, :, cols]
            vb = v_ref[bi, :, cols]
            scores = lax.dot_general(qb, kb, (((1,), (1,)), ((), ())), preferred_element_type=F32)
            inner = (scores * decay_ref[h]).astype(BF16)
            state = state_ref[bi, h]
            out = (jnp.dot(inner, vb, preferred_element_type=F32)
                   + jnp.dot(qb, state.astype(BF16), preferred_element_type=F32) * xi_ref[h])
            vz = (vb.astype(F32) * zeta_ref[h]).astype(BF16)
            state_ref[bi, h] = state * gc_ref[h] + lax.dot_general(
                kb, vz, (((0,), (0,)), ((), ())), preferred_element_type=F32)

            mu = jnp.mean(out, axis=-1, keepdims=True)
            cen = out - mu
            var = jnp.mean(cen * cen, axis=-1, keepdims=True)
            normed = cen * lax.rsqrt(var + GN_EPS) * gain_ref[h]
            gate = g_ref[bi, :, cols].astype(F32)
            o_ref[bi, :, cols] = (normed * (gate * jax.nn.sigmoid(gate))).astype(o_ref.dtype)


def _retention_constants():
    d, c = RET_HEAD_DIM, RET_CHUNK
    log_g = jnp.log1p(-jnp.exp2(-5.0 - jnp.arange(RET_HEADS, dtype=F32)))
    idx = jnp.arange(c, dtype=F32)
    rel = idx[:, None] - idx[None, :]
    decay = jnp.where(rel >= 0, jnp.exp(jnp.maximum(rel, 0.0)[None] * log_g[:, None, None]), 0.0)
    xi = jnp.exp((idx + 1.0)[None] * log_g[:, None])
    zeta = jnp.exp((c - 1.0 - idx)[None] * log_g[:, None])
    g_chunk = jnp.exp(c * log_g)
    bcast = lambda t: jnp.broadcast_to(t[..., None], t.shape + (d,))
    return decay, bcast(xi), bcast(zeta), bcast(g_chunk[:, None])


def _retention(z3d, ret_gn_gain):
    b, s, _ = z3d.shape
    c, d = RET_CHUNK, RET_HEAD_DIM
    base = 4 * ATTN_WIDTH // RET_WIDTH
    decay, xi, zeta, gc = _retention_constants()
    gain = ret_gn_gain.astype(F32).reshape(RET_HEADS, 1, d)
    zspec = lambda off: pl.BlockSpec((b, c, RET_WIDTH), lambda ci: (0, ci, base + off))
    whole = lambda *shape: pl.BlockSpec(shape, lambda ci: (0,) * len(shape))
    return pl.pallas_call(
        _retention_kernel,
        grid=(s // c,),
        in_specs=[zspec(0), zspec(1), zspec(2), zspec(3),
                  whole(RET_HEADS, c, c), whole(RET_HEADS, c, d), whole(RET_HEADS, c, d),
                  whole(RET_HEADS, 1, d), whole(RET_HEADS, 1, d)],
        out_specs=pl.BlockSpec((b, c, RET_WIDTH), lambda ci: (0, ci, 0)),
        out_shape=jax.ShapeDtypeStruct((b, s, RET_WIDTH), BF16),
        scratch_shapes=[pltpu.VMEM((b, RET_HEADS, d, d), F32)],
        compiler_params=pltpu.CompilerParams(dimension_semantics=("arbitrary",),
                                             vmem_limit_bytes=VMEM_LIMIT),
        name="retention",
    )(z3d, z3d, z3d, z3d, decay, xi, zeta, gc, gain)


def _output_kernel(x_ref, a_ref, r_ref, p_ref, wo32_ref, wg32_ref, wp32_ref, gain_ref, bias_ref, o_ref,
                   wo_ref, wg_ref, wp_ref, *, sub):
    @pl.when(pl.program_id(0) == 0)
    def _cast_weights_once():
        wo_ref[...] = wo32_ref[...].astype(BF16)
        wg_ref[...] = wg32_ref[...].astype(BF16)
        wp_ref[...] = wp32_ref[...].astype(BF16)

    for r in range(o_ref.shape[0] // sub):
        rows = slice(r * sub, (r + 1) * sub)
        mix = (jnp.dot(a_ref[rows, :], wo_ref[0:ATTN_WIDTH, :], preferred_element_type=F32)
               + jnp.dot(r_ref[rows, :], wo_ref[ATTN_WIDTH:, :], preferred_element_type=F32))
        u = DEEPNORM_ALPHA * x_ref[rows, :] + mix
        ple = jnp.dot(p_ref[rows, :].astype(BF16), wp_ref[...], preferred_element_type=F32)
        gate = jax.nn.sigmoid(jnp.dot(u.astype(BF16), wg_ref[...], preferred_element_type=F32))
        u = u + gate * ple
        mu = jnp.mean(u, axis=-1, keepdims=True)
        cen = u - mu
        var = jnp.mean(cen * cen, axis=-1, keepdims=True)
        o_ref[rows, :] = cen * lax.rsqrt(var + LN_EPS) * gain_ref[...] + bias_ref[...]


def _output_stage(x2d, attn2d, ret2d, p2d, w_out, w_gate, w_proj, ln_gain, ln_bias, *, tm=1024, sub=256):
    m, dm = x2d.shape
    rows = lambda width: pl.BlockSpec((tm, width), lambda i: (i, 0))
    return pl.pallas_call(
        functools.partial(_output_kernel, sub=sub),
        grid=(m // tm,),
        in_specs=[rows(dm), rows(ATTN_WIDTH), rows(RET_WIDTH), rows(D_PLE),
                  _resident(w_out.shape), _resident(w_gate.shape), _resident(w_proj.shape),
                  _resident((1, dm)), _resident((1, dm))],
        out_specs=rows(dm),
        out_shape=jax.ShapeDtypeStruct((m, dm), F32),
        scratch_shapes=[pltpu.VMEM(w_out.shape, BF16), pltpu.VMEM(w_gate.shape, BF16),
                        pltpu.VMEM(w_proj.shape, BF16)],
        compiler_params=pltpu.CompilerParams(dimension_semantics=("arbitrary",),
                                             vmem_limit_bytes=VMEM_LIMIT),
        name="output_stage",
    )(x2d, attn2d, ret2d, p2d, w_out, w_gate, w_proj,
      ln_gain.astype(F32).reshape(1, dm), ln_bias.astype(F32).reshape(1, dm))


def kernel(x, p, w_in, w_out, ret_gn_gain, w_ple_gate, w_ple_proj, ln_gain, ln_bias):
    b, s, dm = x.shape
    h = x
    for i in range(DEPTH):
        h2d = h.reshape(b * s, dm)
        z = _input_projection(h2d, w_in[i].astype(F32), s)
        z3d = z.reshape(b, s, IN_WIDTH)
        attn = _moba_attention(z3d)
        ret = _retention(z3d, ret_gn_gain[i])
        out = _output_stage(h2d, attn.reshape(b * s, ATTN_WIDTH), ret.reshape(b * s, RET_WIDTH),
                            p[i].reshape(b * s, D_PLE), w_out[i].astype(F32), w_ple_gate[i].astype(F32),
                            w_ple_proj[i].astype(F32), ln_gain[i], ln_bias[i])
        h = out.reshape(b, s, dm)
    return h
```

```python
import functools

import jax
import jax.numpy as jnp
import numpy as np
from jax import lax
from jax.experimental import pallas as pl
from jax.experimental.pallas import tpu as pltpu

F32 = jnp.float32
BF16 = jnp.bfloat16

D_MODEL = 1024
D_PLE = 256
ATTN_HEADS = 8
ATTN_HEAD_DIM = 64
ATTN_WIDTH = ATTN_HEADS * ATTN_HEAD_DIM
RET_HEADS = 4
RET_HEAD_DIM = 128
RET_WIDTH = RET_HEADS * RET_HEAD_DIM
IN_WIDTH = 4 * ATTN_WIDTH + 4 * RET_WIDTH
MOBA_BLOCK = 256
MOBA_TOPK = 3
ROPE_BASE = 10000.0
DEPTH = 1
DEEPNORM_ALPHA = (2.0 * DEPTH) ** 0.25
LN_EPS = 1e-5
GN_EPS = 1e-6

LANES = 128
HEADS_PER_STEP = LANES // ATTN_HEAD_DIM
MASKED = -1e30
LOG2_E = 1.4426950408889634
ONES_ROWS = 16
CHUNK_BLOCKS = 4
RET_CHUNK = 256
VMEM_LIMIT = 48 * 1024 * 1024


def _proj_kernel(x_ref, w32_ref, cos_ref, sin_ref, o_ref, w_ref, *, n_chunk):
    @pl.when(pl.program_id(0) == 0)
    def _cast_weights_once():
        for lo in range(0, w_ref.shape[1], n_chunk):
            w_ref[:, lo:lo + n_chunk] = w32_ref[:, lo:lo + n_chunk].astype(BF16)

    xb = x_ref[...].astype(BF16)
    d = RET_HEAD_DIM
    rot_lo = 4 * ATTN_WIDTH
    rot_mid = rot_lo + RET_WIDTH
    rot_hi = rot_mid + RET_WIDTH
    assert rot_lo % n_chunk == 0 and rot_hi % n_chunk == 0 and n_chunk % d == 0
    for lo in range(0, o_ref.shape[1], n_chunk):
        acc = jnp.dot(xb, w_ref[:, lo:lo + n_chunk], preferred_element_type=F32)
        if rot_lo <= lo < rot_hi:
            cos = cos_ref[...]
            sin = sin_ref[...]
            for col in range(lo, lo + n_chunk, d):
                head = acc[:, col - lo:col - lo + d]
                rotated = head * cos + pltpu.roll(head, d // 2, 1) * sin
                if col >= rot_mid:
                    rotated = rotated * (d ** -0.5)
                o_ref[:, col:col + d] = rotated.astype(o_ref.dtype)
        else:
            o_ref[:, lo:lo + n_chunk] = acc.astype(o_ref.dtype)


def _rotary_tables(s):
    half = RET_HEAD_DIM // 2
    freqs = ROPE_BASE ** (-np.arange(half, dtype=np.float64) / half)
    ang = np.arange(s, dtype=np.float64)[:, None] * freqs[None, :]
    cos, sin = np.cos(ang), np.sin(ang)
    return (np.concatenate([cos, cos], axis=-1).astype(np.float32),
            np.concatenate([-sin, sin], axis=-1).astype(np.float32))


def _resident(shape):
    return pl.BlockSpec(shape, lambda *_: (0,) * len(shape), pipeline_mode=pl.Buffered(1))


def _input_projection(x2d, w_in, seq, *, tm=512, n_chunk=1024):
    m, k = x2d.shape
    n = w_in.shape[1]
    cos, sin = _rotary_tables(seq)
    pos = pl.BlockSpec((tm, RET_HEAD_DIM), lambda i: (i % (seq // tm), 0))
    return pl.pallas_call(
        functools.partial(_proj_kernel, n_chunk=n_chunk),
        grid=(m // tm,),
        in_specs=[pl.BlockSpec((tm, k), lambda i: (i, 0)), _resident((k, n)), pos, pos],
        out_specs=pl.BlockSpec((tm, n), lambda i: (i, 0)),
        out_shape=jax.ShapeDtypeStruct((m, n), BF16),
        scratch_shapes=[pltpu.VMEM((k, n), BF16)],
        compiler_params=pltpu.CompilerParams(dimension_semantics=("arbitrary",),
                                             vmem_limit_bytes=VMEM_LIMIT),
        name="input_projection",
    )(x2d, w_in, cos, sin)


def _moba_kernel(q_ref, k_ref, v_ref, g_ref, onehot_ref, o_ref,
                 vtc_ref, qtb_ref, s0_ref, s1_ref, mrun_ref, mnow_ref, alpha_ref, acc_ref, *, nb):
    blk = MOBA_BLOCK
    hd = ATTN_HEAD_DIM
    nh = HEADS_PER_STEP
    width = nh * blk
    vrows = LANES + ONES_ROWS
    seq = nb * blk
    chunk = CHUNK_BLOCKS * blk
    n_chunks = nb // CHUNK_BLOCKS
    s_refs = (s0_ref, s1_ref)

    @pl.when((pl.program_id(0) == 0) & (pl.program_id(1) == 0))
    def _fill_constant_regions():
        vtc_ref[:, LANES:vrows, :] = jnp.ones((n_chunks, ONES_ROWS, chunk), BF16)
        qtb_ref[...] = jnp.zeros(qtb_ref.shape, BF16)

    def _prepare_head_pair():
        for n in range(nb):
            c, j = divmod(n, CHUNK_BLOCKS)
            vtc_ref[c, 0:LANES, j * blk:(j + 1) * blk] = (
                v_ref[0, n * blk:(n + 1) * blk, :].astype(F32).T.astype(BF16))
        key_blk = lax.broadcasted_iota(jnp.int32, (nb, seq), 1) // blk
        row_blk = lax.broadcasted_iota(jnp.int32, (nb, seq), 0)
        averager = jnp.where(key_blk == row_blk, 1.0 / blk, 0.0).astype(BF16)
        kmean = jnp.dot(averager, k_ref[0], preferred_element_type=F32)
        hi = kmean.astype(BF16)
        lo = (kmean - hi.astype(F32)).astype(BF16)
        lane = lax.broadcasted_iota(jnp.int32, (nb, LANES), 1)
        parts = []
        for h in range(nh):
            in_head = (lane >= h * hd) & (lane < (h + 1) * hd)
            parts += [jnp.where(in_head, hi, jnp.zeros_like(hi)), jnp.where(in_head, lo, jnp.zeros_like(lo))]
        kstack = jnp.concatenate(parts, axis=0)
        qts = []
        for n in range(nb):
            qt = (q_ref[0, n * blk:(n + 1) * blk, :].astype(F32) * (hd ** -0.5 * LOG2_E)).T.astype(BF16)
            for h in range(nh):
                qtb_ref[n, h * hd:(h + 1) * hd, h * blk:(h + 1) * blk] = qt[h * hd:(h + 1) * hd]
            qts.append(qt)
        gates = jnp.dot(kstack, jnp.concatenate(qts, axis=1), preferred_element_type=F32)
        past = row_blk < key_blk
        row_f = row_blk.astype(F32)
        own_f = jnp.where(row_blk == key_blk, 1.0, 0.0)
        for h in range(nh):
            g = gates[(2 * h) * nb:(2 * h + 1) * nb] + gates[(2 * h + 1) * nb:(2 * h + 2) * nb]
            g = jnp.where(past, g, -jnp.inf)
            picked = jnp.zeros((nb, seq), F32)
            for _ in range(MOBA_TOPK):
                top = jnp.max(g, axis=0, keepdims=True)
                first = jnp.min(jnp.where(g == top, row_f, float(nb)), axis=0, keepdims=True)
                pick = row_f == first
                picked = jnp.where(pick, 1.0, picked)
                g = jnp.where(pick, -jnp.inf, g)
            bias = jnp.where(jnp.where(past, picked, own_f) > 0.5, 0.0, MASKED).astype(BF16)
            for n in range(nb):
                qtb_ref[n, LANES:LANES + nb, h * blk:(h + 1) * blk] = bias[:, n * blk:(n + 1) * blk]

    _prepare_head_pair()

    key_pos = lax.broadcasted_iota(jnp.int32, (blk, width), 0)
    q_pos = lax.broadcasted_iota(jnp.int32, (blk, width), 1) & (blk - 1)
    causal = key_pos <= q_pos

    def rows_at(start, size):
        if isinstance(start, int):
            return pl.ds(start, size)
        return pl.ds(pl.multiple_of(start, blk), size)

    def scores(c, qi, slot, cnt, own_j=None):
        key_rows = rows_at(c * chunk, cnt * blk)
        keys_ext = jnp.concatenate([k_ref[0, key_rows, :], onehot_ref[key_rows, :]], axis=1)
        s = jnp.dot(keys_ext, qtb_ref[qi], preferred_element_type=F32)
        m8 = None
        for j in range(cnt):
            sj = s[j * blk:(j + 1) * blk]
            if j == own_j:
                sj = jnp.where(causal, sj, MASKED)
            s_refs[slot][j * blk:(j + 1) * blk, :] = sj
            mj = jnp.max(sj.reshape(blk // 8, 8, width), axis=0)
            m8 = mj if m8 is None else jnp.maximum(m8, mj)
        m_new = jnp.broadcast_to(jnp.max(m8, axis=0, keepdims=True), (8, width))
        if c > 0:
            m_old = mrun_ref[qi]
            m_new = jnp.maximum(m_old, m_new)
            alpha_ref[slot] = jnp.exp2(m_old - m_new)
        mrun_ref[qi] = m_new
        mnow_ref[slot] = m_new

    def values(c, qi, slot, cnt):
        p = jnp.exp2(s_refs[slot][0:cnt * blk, :] - mnow_ref[slot, 0:1, :]).astype(BF16)
        pv = jnp.dot(vtc_ref[c, :, 0:cnt * blk], p, preferred_element_type=F32)
        if c > 0:
            pv = acc_ref[qi] * alpha_ref[slot, 0:1, :] + pv
        acc_ref[qi] = pv

    def finish(qi):
        outs = []
        for h in range(nh):
            cols = slice(h * blk, (h + 1) * blk)
            outs.append(acc_ref[qi, h * hd:(h + 1) * hd, cols] / acc_ref[qi, LANES:LANES + 1, cols])
        attn = jnp.concatenate(outs, axis=0).T
        gate = g_ref[0, rows_at(qi * blk, blk), :].astype(F32)
        o_ref[0, rows_at(qi * blk, blk), :] = (attn * (gate * jax.nn.sigmoid(gate))).astype(o_ref.dtype)

    assert CHUNK_BLOCKS % 2 == 0 and nb % CHUNK_BLOCKS == 0
    for c in range(n_chunks):
        first = c * CHUNK_BLOCKS
        later = first + CHUNK_BLOCKS
        n_pairs = (nb - later) // 2
        scores(c, first, 0, 1, own_j=0)
        for j in range(CHUNK_BLOCKS):
            if j + 1 < CHUNK_BLOCKS:
                scores(c, first + j + 1, (j + 1) % 2, j + 2, own_j=j + 1)
            elif n_pairs > 0:
                scores(c, later, 0, CHUNK_BLOCKS)
            values(c, first + j, j % 2, j + 1)
            finish(first + j)
        if n_pairs == 0:
            continue

        def query_pair(t, carry, c=c, later=later):
            qa = later + 2 * t
            scores(c, qa + 1, 1, CHUNK_BLOCKS)
            values(c, qa, 0, CHUNK_BLOCKS)
            scores(c, qa + 2, 0, CHUNK_BLOCKS)
            values(c, qa + 1, 1, CHUNK_BLOCKS)
            return carry

        lax.fori_loop(0, n_pairs - 1, query_pair, 0)
        scores(c, nb - 1, 1, CHUNK_BLOCKS)
        values(c, nb - 2, 0, CHUNK_BLOCKS)
        values(c, nb - 1, 1, CHUNK_BLOCKS)


def _moba_attention(z3d):
    b, s, _ = z3d.shape
    nb = s // MOBA_BLOCK
    steps = ATTN_WIDTH // LANES
    width = HEADS_PER_STEP * MOBA_BLOCK
    vrows = LANES + ONES_ROWS
    kernel = functools.partial(_moba_kernel, nb=nb)
    full = lambda off: pl.BlockSpec((1, s, LANES), lambda bi, hp: (bi, 0, off * steps + hp))
    onehot = jnp.asarray(np.arange(s)[:, None] // MOBA_BLOCK == np.arange(LANES)[None, :], dtype=BF16)
    return pl.pallas_call(
        kernel,
        grid=(b, steps),
        in_specs=[full(0), full(1), full(2), full(3), _resident((s, LANES))],
        out_specs=pl.BlockSpec((1, s, LANES), lambda bi, hp: (bi, 0, hp)),
        out_shape=jax.ShapeDtypeStruct((b, s, ATTN_WIDTH), BF16),
        scratch_shapes=[pltpu.VMEM((nb // CHUNK_BLOCKS, vrows, CHUNK_BLOCKS * MOBA_BLOCK), BF16),
                        pltpu.VMEM((nb, 2 * LANES, width), BF16),
                        pltpu.VMEM((CHUNK_BLOCKS * MOBA_BLOCK, width), F32),
                        pltpu.VMEM((CHUNK_BLOCKS * MOBA_BLOCK, width), F32),
                        pltpu.VMEM((nb, 8, width), F32),
                        pltpu.VMEM((2, 8, width), F32),
                        pltpu.VMEM((2, 8, width), F32),
                        pltpu.VMEM((nb, vrows, width), F32)],
        compiler_params=pltpu.CompilerParams(dimension_semantics=("arbitrary",) * 2,
                                             vmem_limit_bytes=VMEM_LIMIT),
        name="moba_attention",
    )(z3d, z3d, z3d, z3d, onehot)


def _retention_kernel(q_ref, k_ref, v_ref, g_ref, decay_ref, xi_ref, zeta_ref, gc_ref, gain_ref,
                      o_ref, state_ref):
    c = pl.program_id(0)
    d = RET_HEAD_DIM

    @pl.when(c == 0)
    def _reset_state():
        state_ref[...] = jnp.zeros_like(state_ref)

    for bi in range(q_ref.shape[0]):
        for h in range(RET_HEADS):
            cols = slice(h * d, (h + 1) * d)
            qb = q_ref[bi, :, cols]
            kb = k_ref[bi, :, cols]
            vb = v_ref[bi, :, cols]
            scores = lax.dot_general(qb, kb, (((1,), (1,)), ((), ())), preferred_element_type=F32)
            inner = (scores * decay_ref[h]).astype(BF16)
            state = state_ref[bi, h]
            out = (jnp.dot(inner, vb, preferred_element_type=F32)
                   + jnp.dot(qb, state.astype(BF16), preferred_element_type=F32) * xi_ref[h])
            vz = (vb.astype(F32) * zeta_ref[h]).astype(BF16)
            state_ref[bi, h] = state * gc_ref[h] + lax.dot_general(
                kb, vz, (((0,), (0,)), ((), ())), preferred_element_type=F32)

            mu = jnp.mean(out, axis=-1, keepdims=True)
            cen = out - mu
            var = jnp.mean(cen * cen, axis=-1, keepdims=True)
            normed = cen * lax.rsqrt(var + GN_EPS) * gain_ref[h]
            gate = g_ref[bi, :, cols].astype(F32)
            o_ref[bi, :, cols] = (normed * (gate * jax.nn.sigmoid(gate))).astype(o_ref.dtype)


def _retention_constants():
    d, c = RET_HEAD_DIM, RET_CHUNK
    log_g = np.log1p(-np.exp2(-5.0 - np.arange(RET_HEADS, dtype=np.float64)))
    idx = np.arange(c, dtype=np.float64)
    rel = idx[:, None] - idx[None, :]
    decay = np.where(rel >= 0, np.exp(np.maximum(rel, 0.0)[None] * log_g[:, None, None]), 0.0)
    xi = np.exp((idx + 1.0)[None] * log_g[:, None])
    zeta = np.exp((c - 1.0 - idx)[None] * log_g[:, None])
    g_chunk = np.exp(c * log_g)
    bcast = lambda t: np.ascontiguousarray(np.broadcast_to(t[..., None], t.shape + (d,)), dtype=np.float32)
    return decay.astype(np.float32), bcast(xi), bcast(zeta), bcast(g_chunk[:, None])


def _retention(z3d, ret_gn_gain):
    b, s, _ = z3d.shape
    c, d = RET_CHUNK, RET_HEAD_DIM
    base = 4 * ATTN_WIDTH // RET_WIDTH
    decay, xi, zeta, gc = _retention_constants()
    gain = ret_gn_gain.astype(F32).reshape(RET_HEADS, 1, d)
    zspec = lambda off: pl.BlockSpec((b, c, RET_WIDTH), lambda ci: (0, ci, base + off))
    whole = lambda *shape: pl.BlockSpec(shape, lambda ci: (0,) * len(shape))
    return pl.pallas_call(
        _retention_kernel,
        grid=(s // c,),
        in_specs=[zspec(0), zspec(1), zspec(2), zspec(3),
                  whole(RET_HEADS, c, c), whole(RET_HEADS, c, d), whole(RET_HEADS, c, d),
                  whole(RET_HEADS, 1, d), whole(RET_HEADS, 1, d)],
        out_specs=pl.BlockSpec((b, c, RET_WIDTH), lambda ci: (0, ci, 0)),
        out_shape=jax.ShapeDtypeStruct((b, s, RET_WIDTH), BF16),
        scratch_shapes=[pltpu.VMEM((b, RET_HEADS, d, d), F32)],
        compiler_params=pltpu.CompilerParams(dimension_semantics=("arbitrary",),
                                             vmem_limit_bytes=VMEM_LIMIT),
        name="retention",
    )(z3d, z3d, z3d, z3d, decay, xi, zeta, gc, gain)


def _output_kernel(x_ref, a_ref, r_ref, p_ref, wo32_ref, wg32_ref, wp32_ref, gain_ref, bias_ref, o_ref,
                   wo_ref, wg_ref, wp_ref, *, sub):
    @pl.when(pl.program_id(0) == 0)
    def _cast_weights_once():
        wo_ref[...] = wo32_ref[...].astype(BF16)
        wg_ref[...] = wg32_ref[...].astype(BF16)
        wp_ref[...] = wp32_ref[...].astype(BF16)

    def matmuls(rows):
        mix = (jnp.dot(a_ref[rows, :], wo_ref[0:ATTN_WIDTH, :], preferred_element_type=F32)
               + jnp.dot(r_ref[rows, :], wo_ref[ATTN_WIDTH:, :], preferred_element_type=F32))
        u = DEEPNORM_ALPHA * x_ref[rows, :] + mix
        ple = jnp.dot(p_ref[rows, :].astype(BF16), wp_ref[...], preferred_element_type=F32)
        gate_in = jnp.dot(u.astype(BF16), wg_ref[...], preferred_element_type=F32)
        return u, ple, gate_in

    def tail(rows, u, ple, gate_in):
        u = u + jax.nn.sigmoid(gate_in) * ple
        mu = jnp.mean(u, axis=-1, keepdims=True)
        cen = u - mu
        var = jnp.mean(cen * cen, axis=-1, keepdims=True)
        o_ref[rows, :] = cen * lax.rsqrt(var + LN_EPS) * gain_ref[...] + bias_ref[...]

    n_sub = o_ref.shape[0] // sub
    rows = [slice(r * sub, (r + 1) * sub) for r in range(n_sub)]
    pending = matmuls(rows[0])
    for r in range(n_sub):
        following = matmuls(rows[r + 1]) if r + 1 < n_sub else None
        tail(rows[r], *pending)
        pending = following


def _output_stage(x2d, attn2d, ret2d, p2d, w_out, w_gate, w_proj, ln_gain, ln_bias, *, tm=1024, sub=256):
    m, dm = x2d.shape
    rows = lambda width: pl.BlockSpec((tm, width), lambda i: (i, 0))
    return pl.pallas_call(
        functools.partial(_output_kernel, sub=sub),
        grid=(m // tm,),
        in_specs=[rows(dm), rows(ATTN_WIDTH), rows(RET_WIDTH), rows(D_PLE),
                  _resident(w_out.shape), _resident(w_gate.shape), _resident(w_proj.shape),
                  _resident((1, dm)), _resident((1, dm))],
        out_specs=rows(dm),
        out_shape=jax.ShapeDtypeStruct((m, dm), F32),
        scratch_shapes=[pltpu.VMEM(w_out.shape, BF16), pltpu.VMEM(w_gate.shape, BF16),
                        pltpu.VMEM(w_proj.shape, BF16)],
        compiler_params=pltpu.CompilerParams(dimension_semantics=("arbitrary",),
                                             vmem_limit_bytes=VMEM_LIMIT),
        name="output_stage",
    )(x2d, attn2d, ret2d, p2d, w_out, w_gate, w_proj,
      ln_gain.astype(F32).reshape(1, dm), ln_bias.astype(F32).reshape(1, dm))


def kernel(x, p, w_in, w_out, ret_gn_gain, w_ple_gate, w_ple_proj, ln_gain, ln_bias):
    b, s, dm = x.shape
    h = x
    for i in range(DEPTH):
        h2d = h.reshape(b * s, dm)
        z = _input_projection(h2d, w_in[i].astype(F32), s)
        z3d = z.reshape(b, s, IN_WIDTH)
        attn = _moba_attention(z3d)
        ret = _retention(z3d, ret_gn_gain[i])
        out = _output_stage(h2d, attn.reshape(b * s, ATTN_WIDTH), ret.reshape(b * s, RET_WIDTH),
                            p[i].reshape(b * s, D_PLE), w_out[i].astype(F32), w_ple_gate[i].astype(F32),
                            w_ple_proj[i].astype(F32), ln_gain[i], ln_bias[i])
        h = out.reshape(b, s, dm)
    return h
```

```python
import functools

import jax
import jax.numpy as jnp
import numpy as np
from jax import lax
from jax.experimental import pallas as pl
from jax.experimental.pallas import tpu as pltpu

F32 = jnp.float32
BF16 = jnp.bfloat16

D_MODEL = 1024
D_PLE = 256
ATTN_HEADS = 8
ATTN_HEAD_DIM = 64
ATTN_WIDTH = ATTN_HEADS * ATTN_HEAD_DIM
RET_HEADS = 4
RET_HEAD_DIM = 128
RET_WIDTH = RET_HEADS * RET_HEAD_DIM
IN_WIDTH = 4 * ATTN_WIDTH + 4 * RET_WIDTH
MOBA_BLOCK = 256
MOBA_TOPK = 3
ROPE_BASE = 10000.0
DEPTH = 1
DEEPNORM_ALPHA = (2.0 * DEPTH) ** 0.25
LN_EPS = 1e-5
GN_EPS = 1e-6

LANES = 128
HEADS_PER_STEP = LANES // ATTN_HEAD_DIM
MASKED = -1e30
LOG2_E = 1.4426950408889634
ONES_ROWS = 16
CHUNK_BLOCKS = 4
PAIR_UNROLL = 2
RET_CHUNK = 256
VMEM_LIMIT = 48 * 1024 * 1024


def _proj_kernel(x_ref, w32_ref, cos_ref, sin_ref, o_ref, w_ref, *, n_chunk):
    @pl.when(pl.program_id(0) == 0)
    def _cast_weights_once():
        for lo in range(0, w_ref.shape[1], n_chunk):
            w_ref[:, lo:lo + n_chunk] = w32_ref[:, lo:lo + n_chunk].astype(BF16)

    xb = x_ref[...].astype(BF16)
    d = RET_HEAD_DIM
    rot_lo = 4 * ATTN_WIDTH
    rot_mid = rot_lo + RET_WIDTH
    rot_hi = rot_mid + RET_WIDTH
    assert rot_lo % n_chunk == 0 and rot_hi % n_chunk == 0 and n_chunk % d == 0
    for lo in range(0, o_ref.shape[1], n_chunk):
        acc = jnp.dot(xb, w_ref[:, lo:lo + n_chunk], preferred_element_type=F32)
        if rot_lo <= lo < rot_hi:
            cos = cos_ref[...]
            sin = sin_ref[...]
            for col in range(lo, lo + n_chunk, d):
                head = acc[:, col - lo:col - lo + d]
                rotated = head * cos + pltpu.roll(head, d // 2, 1) * sin
                if col >= rot_mid:
                    rotated = rotated * (d ** -0.5)
                o_ref[:, col:col + d] = rotated.astype(o_ref.dtype)
        else:
            o_ref[:, lo:lo + n_chunk] = acc.astype(o_ref.dtype)


def _rotary_tables(s):
    half = RET_HEAD_DIM // 2
    freqs = ROPE_BASE ** (-np.arange(half, dtype=np.float64) / half)
    ang = np.arange(s, dtype=np.float64)[:, None] * freqs[None, :]
    cos, sin = np.cos(ang), np.sin(ang)
    return (np.concatenate([cos, cos], axis=-1).astype(np.float32),
            np.concatenate([-sin, sin], axis=-1).astype(np.float32))


def _resident(shape):
    return pl.BlockSpec(shape, lambda *_: (0,) * len(shape), pipeline_mode=pl.Buffered(1))


def _input_projection(x2d, w_in, seq, *, tm=512, n_chunk=1024):
    m, k = x2d.shape
    n = w_in.shape[1]
    cos, sin = _rotary_tables(seq)
    pos = pl.BlockSpec((tm, RET_HEAD_DIM), lambda i: (i % (seq // tm), 0))
    return pl.pallas_call(
        functools.partial(_proj_kernel, n_chunk=n_chunk),
        grid=(m // tm,),
        in_specs=[pl.BlockSpec((tm, k), lambda i: (i, 0)), _resident((k, n)), pos, pos],
        out_specs=pl.BlockSpec((tm, n), lambda i: (i, 0)),
        out_shape=jax.ShapeDtypeStruct((m, n), BF16),
        scratch_shapes=[pltpu.VMEM((k, n), BF16)],
        compiler_params=pltpu.CompilerParams(dimension_semantics=("arbitrary",),
                                             vmem_limit_bytes=VMEM_LIMIT),
        name="input_projection",
    )(x2d, w_in, cos, sin)


def _moba_kernel(q_ref, k_ref, v_ref, g_ref, onehot_ref, o_ref,
                 vtc_ref, qtb_ref, s0_ref, s1_ref, mrun_ref, mnow_ref, alpha_ref, acc_ref, *, nb):
    blk = MOBA_BLOCK
    hd = ATTN_HEAD_DIM
    nh = HEADS_PER_STEP
    width = nh * blk
    vrows = LANES + ONES_ROWS
    seq = nb * blk
    chunk = CHUNK_BLOCKS * blk
    n_chunks = nb // CHUNK_BLOCKS
    s_refs = (s0_ref, s1_ref)

    @pl.when((pl.program_id(0) == 0) & (pl.program_id(1) == 0))
    def _fill_constant_regions():
        vtc_ref[:, LANES:vrows, :] = jnp.ones((n_chunks, ONES_ROWS, chunk), BF16)
        qtb_ref[...] = jnp.zeros(qtb_ref.shape, BF16)

    def _prepare_head_pair():
        for n in range(nb):
            c, j = divmod(n, CHUNK_BLOCKS)
            vtc_ref[c, 0:LANES, j * blk:(j + 1) * blk] = (
                v_ref[0, n * blk:(n + 1) * blk, :].astype(F32).T.astype(BF16))
        key_blk = lax.broadcasted_iota(jnp.int32, (nb, seq), 1) // blk
        row_blk = lax.broadcasted_iota(jnp.int32, (nb, seq), 0)
        averager = jnp.where(key_blk == row_blk, 1.0 / blk, 0.0).astype(BF16)
        kmean = jnp.dot(averager, k_ref[0], preferred_element_type=F32)
        hi = kmean.astype(BF16)
        lo = (kmean - hi.astype(F32)).astype(BF16)
        lane = lax.broadcasted_iota(jnp.int32, (nb, LANES), 1)
        parts = []
        for h in range(nh):
            in_head = (lane >= h * hd) & (lane < (h + 1) * hd)
            parts += [jnp.where(in_head, hi, jnp.zeros_like(hi)), jnp.where(in_head, lo, jnp.zeros_like(lo))]
        kstack = jnp.concatenate(parts, axis=0)
        qts = []
        for n in range(nb):
            qt = (q_ref[0, n * blk:(n + 1) * blk, :].astype(F32) * (hd ** -0.5 * LOG2_E)).T.astype(BF16)
            for h in range(nh):
                qtb_ref[n, h * hd:(h + 1) * hd, h * blk:(h + 1) * blk] = qt[h * hd:(h + 1) * hd]
            qts.append(qt)
        gates = jnp.dot(kstack, jnp.concatenate(qts, axis=1), preferred_element_type=F32)
        past = row_blk < key_blk
        row_f = row_blk.astype(F32)
        own_f = jnp.where(row_blk == key_blk, 1.0, 0.0)
        for h in range(nh):
            g = gates[(2 * h) * nb:(2 * h + 1) * nb] + gates[(2 * h + 1) * nb:(2 * h + 2) * nb]
            g = jnp.where(past, g, -jnp.inf)
            picked = jnp.zeros((nb, seq), F32)
            for _ in range(MOBA_TOPK):
                top = jnp.max(g, axis=0, keepdims=True)
                first = jnp.min(jnp.where(g == top, row_f, float(nb)), axis=0, keepdims=True)
                pick = row_f == first
                picked = jnp.where(pick, 1.0, picked)
                g = jnp.where(pick, -jnp.inf, g)
            bias = jnp.where(jnp.where(past, picked, own_f) > 0.5, 0.0, MASKED).astype(BF16)
            for n in range(nb):
                qtb_ref[n, LANES:LANES + nb, h * blk:(h + 1) * blk] = bias[:, n * blk:(n + 1) * blk]

    _prepare_head_pair()

    key_pos = lax.broadcasted_iota(jnp.int32, (blk, width), 0)
    q_pos = lax.broadcasted_iota(jnp.int32, (blk, width), 1) & (blk - 1)
    causal = key_pos <= q_pos

    def rows_at(start, size):
        if isinstance(start, int):
            return pl.ds(start, size)
        return pl.ds(pl.multiple_of(start, blk), size)

    def scores(c, qi, slot, cnt, own_j=None):
        key_rows = rows_at(c * chunk, cnt * blk)
        keys_ext = jnp.concatenate([k_ref[0, key_rows, :], onehot_ref[key_rows, :]], axis=1)
        s = jnp.dot(keys_ext, qtb_ref[qi], preferred_element_type=F32)
        m8 = None
        for j in range(cnt):
            sj = s[j * blk:(j + 1) * blk]
            if j == own_j:
                sj = jnp.where(causal, sj, MASKED)
            s_refs[slot][j * blk:(j + 1) * blk, :] = sj
            mj = jnp.max(sj.reshape(blk // 8, 8, width), axis=0)
            m8 = mj if m8 is None else jnp.maximum(m8, mj)
        m_new = jnp.broadcast_to(jnp.max(m8, axis=0, keepdims=True), (8, width))
        if c > 0:
            m_old = mrun_ref[qi]
            m_new = jnp.maximum(m_old, m_new)
            alpha_ref[slot] = jnp.exp2(m_old - m_new)
        mrun_ref[qi] = m_new
        mnow_ref[slot] = m_new

    def values(c, qi, slot, cnt):
        p = jnp.exp2(s_refs[slot][0:cnt * blk, :] - mnow_ref[slot, 0:1, :]).astype(BF16)
        pv = jnp.dot(vtc_ref[c, :, 0:cnt * blk], p, preferred_element_type=F32)
        if c > 0:
            pv = acc_ref[qi] * alpha_ref[slot, 0:1, :] + pv
        acc_ref[qi] = pv

    def finish(qi):
        outs = []
        for h in range(nh):
            cols = slice(h * blk, (h + 1) * blk)
            outs.append(acc_ref[qi, h * hd:(h + 1) * hd, cols] / acc_ref[qi, LANES:LANES + 1, cols])
        attn = jnp.concatenate(outs, axis=0).T
        gate = g_ref[0, rows_at(qi * blk, blk), :].astype(F32)
        o_ref[0, rows_at(qi * blk, blk), :] = (attn * (gate * jax.nn.sigmoid(gate))).astype(o_ref.dtype)

    assert CHUNK_BLOCKS % 2 == 0 and nb % CHUNK_BLOCKS == 0
    for c in range(n_chunks):
        first = c * CHUNK_BLOCKS
        later = first + CHUNK_BLOCKS
        n_pairs = (nb - later) // 2
        scores(c, first, 0, 1, own_j=0)
        for j in range(CHUNK_BLOCKS):
            if j + 1 < CHUNK_BLOCKS:
                scores(c, first + j + 1, (j + 1) % 2, j + 2, own_j=j + 1)
            elif n_pairs > 0:
                scores(c, later, 0, CHUNK_BLOCKS)
            values(c, first + j, j % 2, j + 1)
            finish(first + j)
        if n_pairs == 0:
            continue

        group = 2 * PAIR_UNROLL
        assert (nb - later) % group == 0

        def query_group(start, last):
            for i in range(group):
                if not (last and i == group - 1):
                    scores(c, start + i + 1, (i + 1) % 2, CHUNK_BLOCKS)
                values(c, start + i, i % 2, CHUNK_BLOCKS)

        def loop_body(t, carry, later=later):
            query_group(later + group * t, last=False)
            return carry

        n_groups = (nb - later) // group
        lax.fori_loop(0, n_groups - 1, loop_body, 0)
        query_group(nb - group, last=True)


def _moba_attention(z3d):
    b, s, _ = z3d.shape
    nb = s // MOBA_BLOCK
    steps = ATTN_WIDTH // LANES
    width = HEADS_PER_STEP * MOBA_BLOCK
    vrows = LANES + ONES_ROWS
    kernel = functools.partial(_moba_kernel, nb=nb)
    full = lambda off: pl.BlockSpec((1, s, LANES), lambda bi, hp: (bi, 0, off * steps + hp))
    onehot = jnp.asarray(np.arange(s)[:, None] // MOBA_BLOCK == np.arange(LANES)[None, :], dtype=BF16)
    return pl.pallas_call(
        kernel,
        grid=(b, steps),
        in_specs=[full(0), full(1), full(2), full(3), _resident((s, LANES))],
        out_specs=pl.BlockSpec((1, s, LANES), lambda bi, hp: (bi, 0, hp)),
        out_shape=jax.ShapeDtypeStruct((b, s, ATTN_WIDTH), BF16),
        scratch_shapes=[pltpu.VMEM((nb // CHUNK_BLOCKS, vrows, CHUNK_BLOCKS * MOBA_BLOCK), BF16),
                        pltpu.VMEM((nb, 2 * LANES, width), BF16),
                        pltpu.VMEM((CHUNK_BLOCKS * MOBA_BLOCK, width), F32),
                        pltpu.VMEM((CHUNK_BLOCKS * MOBA_BLOCK, width), F32),
                        pltpu.VMEM((nb, 8, width), F32),
                        pltpu.VMEM((2, 8, width), F32),
                        pltpu.VMEM((2, 8, width), F32),
                        pltpu.VMEM((nb, vrows, width), F32)],
        compiler_params=pltpu.CompilerParams(dimension_semantics=("arbitrary",) * 2,
                                             vmem_limit_bytes=VMEM_LIMIT),
        name="moba_attention",
    )(z3d, z3d, z3d, z3d, onehot)


def _retention_kernel(q_ref, k_ref, v_ref, g_ref, decay_ref, xi_ref, zeta_ref, gc_ref, gain_ref,
                      o_ref, state_ref):
    c = pl.program_id(0)
    d = RET_HEAD_DIM

    @pl.when(c == 0)
    def _reset_state():
        state_ref[...] = jnp.zeros_like(state_ref)

    for bi in range(q_ref.shape[0]):
        for h in range(RET_HEADS):
            cols = slice(h * d, (h + 1) * d)
            qb = q_ref[bi, :, cols]
            kb = k_ref[bi, :, cols]
            vb = v_ref[bi, :, cols]
            scores = lax.dot_general(qb, kb, (((1,), (1,)), ((), ())), preferred_element_type=F32)
            inner = (scores * decay_ref[h]).astype(BF16)
            state = state_ref[bi, h]
            out = (jnp.dot(inner, vb, preferred_element_type=F32)
                   + jnp.dot(qb, state.astype(BF16), preferred_element_type=F32) * xi_ref[h])
            vz = (vb.astype(F32) * zeta_ref[h]).astype(BF16)
            state_ref[bi, h] = state * gc_ref[h] + lax.dot_general(
                kb, vz, (((0,), (0,)), ((), ())), preferred_element_type=F32)

            mu = jnp.mean(out, axis=-1, keepdims=True)
            cen = out - mu
            var = jnp.mean(cen * cen, axis=-1, keepdims=True)
            normed = cen * lax.rsqrt(var + GN_EPS) * gain_ref[h]
            gate = g_ref[bi, :, cols].astype(F32)
            o_ref[bi, :, cols] = (normed * (gate * jax.nn.sigmoid(gate))).astype(o_ref.dtype)


def _retention_constants():
    d, c = RET_HEAD_DIM, RET_CHUNK
    log_g = np.log1p(-np.exp2(-5.0 - np.arange(RET_HEADS, dtype=np.float64)))
    idx = np.arange(c, dtype=np.float64)
    rel = idx[:, None] - idx[None, :]
    decay = np.where(rel >= 0, np.exp(np.maximum(rel, 0.0)[None] * log_g[:, None, None]), 0.0)
    xi = np.exp((idx + 1.0)[None] * log_g[:, None])
    zeta = np.exp((c - 1.0 - idx)[None] * log_g[:, None])
    g_chunk = np.exp(c * log_g)
    bcast = lambda t: np.ascontiguousarray(np.broadcast_to(t[..., None], t.shape + (d,)), dtype=np.float32)
    return decay.astype(np.float32), bcast(xi), bcast(zeta), bcast(g_chunk[:, None])


def _retention(z3d, ret_gn_gain):
    b, s, _ = z3d.shape
    c, d = RET_CHUNK, RET_HEAD_DIM
    base = 4 * ATTN_WIDTH // RET_WIDTH
    decay, xi, zeta, gc = _retention_constants()
    gain = ret_gn_gain.astype(F32).reshape(RET_HEADS, 1, d)
    zspec = lambda off: pl.BlockSpec((b, c, RET_WIDTH), lambda ci: (0, ci, base + off))
    whole = lambda *shape: pl.BlockSpec(shape, lambda ci: (0,) * len(shape))
    return pl.pallas_call(
        _retention_kernel,
        grid=(s // c,),
        in_specs=[zspec(0), zspec(1), zspec(2), zspec(3),
                  whole(RET_HEADS, c, c), whole(RET_HEADS, c, d), whole(RET_HEADS, c, d),
                  whole(RET_HEADS, 1, d), whole(RET_HEADS, 1, d)],
        out_specs=pl.BlockSpec((b, c, RET_WIDTH), lambda ci: (0, ci, 0)),
        out_shape=jax.ShapeDtypeStruct((b, s, RET_WIDTH), BF16),
        scratch_shapes=[pltpu.VMEM((b, RET_HEADS, d, d), F32)],
        compiler_params=pltpu.CompilerParams(dimension_semantics=("arbitrary",),
                                             vmem_limit_bytes=VMEM_LIMIT),
        name="retention",
    )(z3d, z3d, z3d, z3d, decay, xi, zeta, gc, gain)


def _output_kernel(x_ref, a_ref, r_ref, p_ref, wo32_ref, wg32_ref, wp32_ref, gain_ref, bias_ref, o_ref,
                   wo_ref, wg_ref, wp_ref, *, sub):
    @pl.when(pl.program_id(0) == 0)
    def _cast_weights_once():
        wo_ref[...] = wo32_ref[...].astype(BF16)
        wg_ref[...] = wg32_ref[...].astype(BF16)
        wp_ref[...] = wp32_ref[...].astype(BF16)

    def matmuls(rows):
        mix = (jnp.dot(a_ref[rows, :], wo_ref[0:ATTN_WIDTH, :], preferred_element_type=F32)
               + jnp.dot(r_ref[rows, :], wo_ref[ATTN_WIDTH:, :], preferred_element_type=F32))
        u = DEEPNORM_ALPHA * x_ref[rows, :] + mix
        ple = jnp.dot(p_ref[rows, :].astype(BF16), wp_ref[...], preferred_element_type=F32)
        gate_in = jnp.dot(u.astype(BF16), wg_ref[...], preferred_element_type=F32)
        return u, ple, gate_in

    def tail(rows, u, ple, gate_in):
        u = u + jax.nn.sigmoid(gate_in) * ple
        mu = jnp.mean(u, axis=-1, keepdims=True)
        cen = u - mu
        var = jnp.mean(cen * cen, axis=-1, keepdims=True)
        o_ref[rows, :] = cen * lax.rsqrt(var + LN_EPS) * gain_ref[...] + bias_ref[...]

    n_sub = o_ref.shape[0] // sub
    rows = [slice(r * sub, (r + 1) * sub) for r in range(n_sub)]
    pending = matmuls(rows[0])
    for r in range(n_sub):
        following = matmuls(rows[r + 1]) if r + 1 < n_sub else None
        tail(rows[r], *pending)
        pending = following


def _output_stage(x2d, attn2d, ret2d, p2d, w_out, w_gate, w_proj, ln_gain, ln_bias, *, tm=1024, sub=256):
    m, dm = x2d.shape
    rows = lambda width: pl.BlockSpec((tm, width), lambda i: (i, 0))
    return pl.pallas_call(
        functools.partial(_output_kernel, sub=sub),
        grid=(m // tm,),
        in_specs=[rows(dm), rows(ATTN_WIDTH), rows(RET_WIDTH), rows(D_PLE),
                  _resident(w_out.shape), _resident(w_gate.shape), _resident(w_proj.shape),
                  _resident((1, dm)), _resident((1, dm))],
        out_specs=rows(dm),
        out_shape=jax.ShapeDtypeStruct((m, dm), F32),
        scratch_shapes=[pltpu.VMEM(w_out.shape, BF16), pltpu.VMEM(w_gate.shape, BF16),
                        pltpu.VMEM(w_proj.shape, BF16)],
        compiler_params=pltpu.CompilerParams(dimension_semantics=("arbitrary",),
                                             vmem_limit_bytes=VMEM_LIMIT),
        name="output_stage",
    )(x2d, attn2d, ret2d, p2d, w_out, w_gate, w_proj,
      ln_gain.astype(F32).reshape(1, dm), ln_bias.astype(F32).reshape(1, dm))


def kernel(x, p, w_in, w_out, ret_gn_gain, w_ple_gate, w_ple_proj, ln_gain, ln_bias):
    b, s, dm = x.shape
    h = x
    for i in range(DEPTH):
        h2d = h.reshape(b * s, dm)
        z = _input_projection(h2d, w_in[i].astype(F32), s)
        z3d = z.reshape(b, s, IN_WIDTH)
        attn = _moba_attention(z3d)
        ret = _retention(z3d, ret_gn_gain[i])
        out = _output_stage(h2d, attn.reshape(b * s, ATTN_WIDTH), ret.reshape(b * s, RET_WIDTH),
                            p[i].reshape(b * s, D_PLE), w_out[i].astype(F32), w_ple_gate[i].astype(F32),
                            w_ple_proj[i].astype(F32), ln_gain[i], ln_bias[i])
        h = out.reshape(b, s, dm)
    return h
```

```python
import functools

import jax
import jax.numpy as jnp
import numpy as np
from jax import lax
from jax.experimental import pallas as pl
from jax.experimental.pallas import tpu as pltpu

F32 = jnp.float32
BF16 = jnp.bfloat16

D_MODEL = 1024
D_PLE = 256
ATTN_HEADS = 8
ATTN_HEAD_DIM = 64
ATTN_WIDTH = ATTN_HEADS * ATTN_HEAD_DIM
RET_HEADS = 4
RET_HEAD_DIM = 128
RET_WIDTH = RET_HEADS * RET_HEAD_DIM
IN_WIDTH = 4 * ATTN_WIDTH + 4 * RET_WIDTH
MOBA_BLOCK = 256
MOBA_TOPK = 3
ROPE_BASE = 10000.0
DEPTH = 1
DEEPNORM_ALPHA = (2.0 * DEPTH) ** 0.25
LN_EPS = 1e-5
GN_EPS = 1e-6

LANES = 128
HEADS_PER_STEP = LANES // ATTN_HEAD_DIM
MASKED = -1e30
LOG2_E = 1.4426950408889634
ONES_ROWS = 16
CHUNK_BLOCKS = 4
PAIR_UNROLL = 2
RET_CHUNK = 256
VMEM_LIMIT = 48 * 1024 * 1024


def _proj_kernel(x_ref, w32_ref, cos_ref, sin_ref, o_ref, w_ref, *, n_chunk):
    @pl.when(pl.program_id(0) == 0)
    def _cast_weights_once():
        for lo in range(0, w_ref.shape[1], n_chunk):
            w_ref[:, lo:lo + n_chunk] = w32_ref[:, lo:lo + n_chunk].astype(BF16)

    xb = x_ref[...].astype(BF16)
    d = RET_HEAD_DIM
    rot_lo = 4 * ATTN_WIDTH
    rot_mid = rot_lo + RET_WIDTH
    rot_hi = rot_mid + RET_WIDTH
    assert rot_lo % n_chunk == 0 and rot_hi % n_chunk == 0 and n_chunk % d == 0
    for lo in range(0, o_ref.shape[1], n_chunk):
        acc = jnp.dot(xb, w_ref[:, lo:lo + n_chunk], preferred_element_type=F32)
        if rot_lo <= lo < rot_hi:
            cos = cos_ref[...]
            sin = sin_ref[...]
            for col in range(lo, lo + n_chunk, d):
                head = acc[:, col - lo:col - lo + d]
                rotated = head * cos + pltpu.roll(head, d // 2, 1) * sin
                if col >= rot_mid:
                    rotated = rotated * (d ** -0.5)
                o_ref[:, col:col + d] = rotated.astype(o_ref.dtype)
        else:
            o_ref[:, lo:lo + n_chunk] = acc.astype(o_ref.dtype)


def _rotary_tables(s):
    half = RET_HEAD_DIM // 2
    freqs = ROPE_BASE ** (-np.arange(half, dtype=np.float64) / half)
    ang = np.arange(s, dtype=np.float64)[:, None] * freqs[None, :]
    cos, sin = np.cos(ang), np.sin(ang)
    return (np.concatenate([cos, cos], axis=-1).astype(np.float32),
            np.concatenate([-sin, sin], axis=-1).astype(np.float32))


def _resident(shape):
    return pl.BlockSpec(shape, lambda *_: (0,) * len(shape), pipeline_mode=pl.Buffered(1))


def _input_projection(x2d, w_in, seq, *, tm=512, n_chunk=1024):
    m, k = x2d.shape
    n = w_in.shape[1]
    cos, sin = _rotary_tables(seq)
    pos = pl.BlockSpec((tm, RET_HEAD_DIM), lambda i: (i % (seq // tm), 0))
    return pl.pallas_call(
        functools.partial(_proj_kernel, n_chunk=n_chunk),
        grid=(m // tm,),
        in_specs=[pl.BlockSpec((tm, k), lambda i: (i, 0)), _resident((k, n)), pos, pos],
        out_specs=pl.BlockSpec((tm, n), lambda i: (i, 0)),
        out_shape=jax.ShapeDtypeStruct((m, n), BF16),
        scratch_shapes=[pltpu.VMEM((k, n), BF16)],
        compiler_params=pltpu.CompilerParams(dimension_semantics=("arbitrary",),
                                             vmem_limit_bytes=VMEM_LIMIT),
        name="input_projection",
    )(x2d, w_in, cos, sin)


def _moba_kernel(q_ref, k_ref, v_ref, g_ref, onehot_ref, o_ref,
                 vtc_ref, qtb_ref, s0_ref, s1_ref, mrun_ref, mnow_ref, alpha_ref, acc_ref, *, nb):
    blk = MOBA_BLOCK
    hd = ATTN_HEAD_DIM
    nh = HEADS_PER_STEP
    width = nh * blk
    vrows = hd + ONES_ROWS
    seq = nb * blk
    chunk = CHUNK_BLOCKS * blk
    n_chunks = nb // CHUNK_BLOCKS
    s_refs = (s0_ref, s1_ref)

    @pl.when((pl.program_id(0) == 0) & (pl.program_id(1) == 0))
    def _fill_constant_regions():
        vtc_ref[:, :, hd:vrows, :] = jnp.ones((n_chunks, nh, ONES_ROWS, chunk), BF16)
        qtb_ref[...] = jnp.zeros(qtb_ref.shape, BF16)

    def _prepare_head_pair():
        for n in range(nb):
            c, j = divmod(n, CHUNK_BLOCKS)
            vt = v_ref[0, n * blk:(n + 1) * blk, :].astype(F32).T.astype(BF16)
            for h in range(nh):
                vtc_ref[c, h, 0:hd, j * blk:(j + 1) * blk] = vt[h * hd:(h + 1) * hd]
        key_blk = lax.broadcasted_iota(jnp.int32, (nb, seq), 1) // blk
        row_blk = lax.broadcasted_iota(jnp.int32, (nb, seq), 0)
        averager = jnp.where(key_blk == row_blk, 1.0 / blk, 0.0).astype(BF16)
        kmean = jnp.dot(averager, k_ref[0], preferred_element_type=F32)
        hi = kmean.astype(BF16)
        lo = (kmean - hi.astype(F32)).astype(BF16)
        lane = lax.broadcasted_iota(jnp.int32, (nb, LANES), 1)
        parts = []
        for h in range(nh):
            in_head = (lane >= h * hd) & (lane < (h + 1) * hd)
            parts += [jnp.where(in_head, hi, jnp.zeros_like(hi)), jnp.where(in_head, lo, jnp.zeros_like(lo))]
        kstack = jnp.concatenate(parts, axis=0)
        qts = []
        for n in range(nb):
            qt = (q_ref[0, n * blk:(n + 1) * blk, :].astype(F32) * (hd ** -0.5 * LOG2_E)).T.astype(BF16)
            for h in range(nh):
                qtb_ref[n, h * hd:(h + 1) * hd, h * blk:(h + 1) * blk] = qt[h * hd:(h + 1) * hd]
            qts.append(qt)
        gates = jnp.dot(kstack, jnp.concatenate(qts, axis=1), preferred_element_type=F32)
        past = row_blk < key_blk
        row_f = row_blk.astype(F32)
        own_f = jnp.where(row_blk == key_blk, 1.0, 0.0)
        for h in range(nh):
            g = gates[(2 * h) * nb:(2 * h + 1) * nb] + gates[(2 * h + 1) * nb:(2 * h + 2) * nb]
            g = jnp.where(past, g, -jnp.inf)
            picked = jnp.zeros((nb, seq), F32)
            for _ in range(MOBA_TOPK):
                top = jnp.max(g, axis=0, keepdims=True)
                first = jnp.min(jnp.where(g == top, row_f, float(nb)), axis=0, keepdims=True)
                pick = row_f == first
                picked = jnp.where(pick, 1.0, picked)
                g = jnp.where(pick, -jnp.inf, g)
            bias = jnp.where(jnp.where(past, picked, own_f) > 0.5, 0.0, MASKED).astype(BF16)
            for n in range(nb):
                qtb_ref[n, LANES:LANES + nb, h * blk:(h + 1) * blk] = bias[:, n * blk:(n + 1) * blk]

    _prepare_head_pair()

    key_pos = lax.broadcasted_iota(jnp.int32, (blk, width), 0)
    q_pos = lax.broadcasted_iota(jnp.int32, (blk, width), 1) & (blk - 1)
    causal = key_pos <= q_pos

    def rows_at(start, size):
        if isinstance(start, int):
            return pl.ds(start, size)
        return pl.ds(pl.multiple_of(start, blk), size)

    def scores(c, qi, slot, cnt, own_j=None):
        key_rows = rows_at(c * chunk, cnt * blk)
        keys_ext = jnp.concatenate([k_ref[0, key_rows, :], onehot_ref[key_rows, :]], axis=1)
        s = jnp.dot(keys_ext, qtb_ref[qi], preferred_element_type=F32)
        m8 = None
        for j in range(cnt):
            sj = s[j * blk:(j + 1) * blk]
            if j == own_j:
                sj = jnp.where(causal, sj, MASKED)
            s_refs[slot][j * blk:(j + 1) * blk, :] = sj
            mj = jnp.max(sj.reshape(blk // 8, 8, width), axis=0)
            m8 = mj if m8 is None else jnp.maximum(m8, mj)
        m_new = jnp.broadcast_to(jnp.max(m8, axis=0, keepdims=True), (8, width))
        if c > 0:
            m_old = mrun_ref[qi]
            m_new = jnp.maximum(m_old, m_new)
            alpha_ref[slot] = jnp.exp2(m_old - m_new)
        mrun_ref[qi] = m_new
        mnow_ref[slot] = m_new

    def values(c, qi, slot, cnt):
        p = jnp.exp2(s_refs[slot][0:cnt * blk, :] - mnow_ref[slot, 0:1, :]).astype(BF16)
        for h in range(nh):
            cols = slice(h * blk, (h + 1) * blk)
            pv = jnp.dot(vtc_ref[c, h, :, 0:cnt * blk], p[:, cols], preferred_element_type=F32)
            if c > 0:
                pv = acc_ref[qi, h] * alpha_ref[slot, 0:1, cols] + pv
            acc_ref[qi, h] = pv

    def finish(qi):
        outs = [acc_ref[qi, h, 0:hd, :] / acc_ref[qi, h, hd:hd + 1, :] for h in range(nh)]
        attn = jnp.concatenate(outs, axis=0).T
        gate = g_ref[0, rows_at(qi * blk, blk), :].astype(F32)
        o_ref[0, rows_at(qi * blk, blk), :] = (attn * (gate * jax.nn.sigmoid(gate))).astype(o_ref.dtype)

    assert CHUNK_BLOCKS % 2 == 0 and nb % CHUNK_BLOCKS == 0
    for c in range(n_chunks):
        first = c * CHUNK_BLOCKS
        later = first + CHUNK_BLOCKS
        n_pairs = (nb - later) // 2
        scores(c, first, 0, 1, own_j=0)
        for j in range(CHUNK_BLOCKS):
            if j + 1 < CHUNK_BLOCKS:
                scores(c, first + j + 1, (j + 1) % 2, j + 2, own_j=j + 1)
            elif n_pairs > 0:
                scores(c, later, 0, CHUNK_BLOCKS)
            values(c, first + j, j % 2, j + 1)
            finish(first + j)
        if n_pairs == 0:
            continue

        group = 2 * PAIR_UNROLL
        assert (nb - later) % group == 0

        def query_group(start, last):
            for i in range(group):
                if not (last and i == group - 1):
                    scores(c, start + i + 1, (i + 1) % 2, CHUNK_BLOCKS)
                values(c, start + i, i % 2, CHUNK_BLOCKS)

        def loop_body(t, carry, later=later):
            query_group(later + group * t, last=False)
            return carry

        n_groups = (nb - later) // group
        lax.fori_loop(0, n_groups - 1, loop_body, 0)
        query_group(nb - group, last=True)


def _moba_attention(z3d):
    b, s, _ = z3d.shape
    nb = s // MOBA_BLOCK
    steps = ATTN_WIDTH // LANES
    width = HEADS_PER_STEP * MOBA_BLOCK
    nh = HEADS_PER_STEP
    vrows = ATTN_HEAD_DIM + ONES_ROWS
    kernel = functools.partial(_moba_kernel, nb=nb)
    full = lambda off: pl.BlockSpec((1, s, LANES), lambda bi, hp: (bi, 0, off * steps + hp))
    onehot = jnp.asarray(np.arange(s)[:, None] // MOBA_BLOCK == np.arange(LANES)[None, :], dtype=BF16)
    return pl.pallas_call(
        kernel,
        grid=(b, steps),
        in_specs=[full(0), full(1), full(2), full(3), _resident((s, LANES))],
        out_specs=pl.BlockSpec((1, s, LANES), lambda bi, hp: (bi, 0, hp)),
        out_shape=jax.ShapeDtypeStruct((b, s, ATTN_WIDTH), BF16),
        scratch_shapes=[pltpu.VMEM((nb // CHUNK_BLOCKS, nh, vrows, CHUNK_BLOCKS * MOBA_BLOCK), BF16),
                        pltpu.VMEM((nb, 2 * LANES, width), BF16),
                        pltpu.VMEM((CHUNK_BLOCKS * MOBA_BLOCK, width), F32),
                        pltpu.VMEM((CHUNK_BLOCKS * MOBA_BLOCK, width), F32),
                        pltpu.VMEM((nb, 8, width), F32),
                        pltpu.VMEM((2, 8, width), F32),
                        pltpu.VMEM((2, 8, width), F32),
                        pltpu.VMEM((nb, nh, vrows, MOBA_BLOCK), F32)],
        compiler_params=pltpu.CompilerParams(dimension_semantics=("arbitrary",) * 2,
                                             vmem_limit_bytes=VMEM_LIMIT),
        name="moba_attention",
    )(z3d, z3d, z3d, z3d, onehot)


def _retention_kernel(q_ref, k_ref, v_ref, g_ref, decay_ref, xi_ref, zeta_ref, gc_ref, gain_ref,
                      o_ref, state_ref):
    c = pl.program_id(0)
    d = RET_HEAD_DIM

    @pl.when(c == 0)
    def _reset_state():
        state_ref[...] = jnp.zeros_like(state_ref)

    for bi in range(q_ref.shape[0]):
        for h in range(RET_HEADS):
            cols = slice(h * d, (h + 1) * d)
            qb = q_ref[bi, :, cols]
            kb = k_ref[bi, :, cols]
            vb = v_ref[bi, :, cols]
            scores = lax.dot_general(qb, kb, (((1,), (1,)), ((), ())), preferred_element_type=F32)
            inner = (scores * decay_ref[h]).astype(BF16)
            state = state_ref[bi, h]
            out = (jnp.dot(inner, vb, preferred_element_type=F32)
                   + jnp.dot(qb, state.astype(BF16), preferred_element_type=F32) * xi_ref[h])
            vz = (vb.astype(F32) * zeta_ref[h]).astype(BF16)
            state_ref[bi, h] = state * gc_ref[h] + lax.dot_general(
                kb, vz, (((0,), (0,)), ((), ())), preferred_element_type=F32)

            mu = jnp.mean(out, axis=-1, keepdims=True)
            cen = out - mu
            var = jnp.mean(cen * cen, axis=-1, keepdims=True)
            normed = cen * lax.rsqrt(var + GN_EPS) * gain_ref[h]
            gate = g_ref[bi, :, cols].astype(F32)
            o_ref[bi, :, cols] = (normed * (gate * jax.nn.sigmoid(gate))).astype(o_ref.dtype)


def _retention_constants():
    d, c = RET_HEAD_DIM, RET_CHUNK
    log_g = np.log1p(-np.exp2(-5.0 - np.arange(RET_HEADS, dtype=np.float64)))
    idx = np.arange(c, dtype=np.float64)
    rel = idx[:, None] - idx[None, :]
    decay = np.where(rel >= 0, np.exp(np.maximum(rel, 0.0)[None] * log_g[:, None, None]), 0.0)
    xi = np.exp((idx + 1.0)[None] * log_g[:, None])
    zeta = np.exp((c - 1.0 - idx)[None] * log_g[:, None])
    g_chunk = np.exp(c * log_g)
    bcast = lambda t: np.ascontiguousarray(np.broadcast_to(t[..., None], t.shape + (d,)), dtype=np.float32)
    return decay.astype(np.float32), bcast(xi), bcast(zeta), bcast(g_chunk[:, None])


def _retention(z3d, ret_gn_gain):
    b, s, _ = z3d.shape
    c, d = RET_CHUNK, RET_HEAD_DIM
    base = 4 * ATTN_WIDTH // RET_WIDTH
    decay, xi, zeta, gc = _retention_constants()
    gain = ret_gn_gain.astype(F32).reshape(RET_HEADS, 1, d)
    zspec = lambda off: pl.BlockSpec((b, c, RET_WIDTH), lambda ci: (0, ci, base + off))
    whole = lambda *shape: pl.BlockSpec(shape, lambda ci: (0,) * len(shape))
    return pl.pallas_call(
        _retention_kernel,
        grid=(s // c,),
        in_specs=[zspec(0), zspec(1), zspec(2), zspec(3),
                  whole(RET_HEADS, c, c), whole(RET_HEADS, c, d), whole(RET_HEADS, c, d),
                  whole(RET_HEADS, 1, d), whole(RET_HEADS, 1, d)],
        out_specs=pl.BlockSpec((b, c, RET_WIDTH), lambda ci: (0, ci, 0)),
        out_shape=jax.ShapeDtypeStruct((b, s, RET_WIDTH), BF16),
        scratch_shapes=[pltpu.VMEM((b, RET_HEADS, d, d), F32)],
        compiler_params=pltpu.CompilerParams(dimension_semantics=("arbitrary",),
                                             vmem_limit_bytes=VMEM_LIMIT),
        name="retention",
    )(z3d, z3d, z3d, z3d, decay, xi, zeta, gc, gain)


def _output_kernel(x_ref, a_ref, r_ref, p_ref, wo32_ref, wg32_ref, wp32_ref, gain_ref, bias_ref, o_ref,
                   wo_ref, wg_ref, wp_ref, *, sub):
    @pl.when(pl.program_id(0) == 0)
    def _cast_weights_once():
        wo_ref[...] = wo32_ref[...].astype(BF16)
        wg_ref[...] = wg32_ref[...].astype(BF16)
        wp_ref[...] = wp32_ref[...].astype(BF16)

    def matmuls(rows):
        mix = (jnp.dot(a_ref[rows, :], wo_ref[0:ATTN_WIDTH, :], preferred_element_type=F32)
               + jnp.dot(r_ref[rows, :], wo_ref[ATTN_WIDTH:, :], preferred_element_type=F32))
        u = DEEPNORM_ALPHA * x_ref[rows, :] + mix
        ple = jnp.dot(p_ref[rows, :].astype(BF16), wp_ref[...], preferred_element_type=F32)
        gate_in = jnp.dot(u.astype(BF16), wg_ref[...], preferred_element_type=F32)
        return u, ple, gate_in

    def tail(rows, u, ple, gate_in):
        u = u + jax.nn.sigmoid(gate_in) * ple
        mu = jnp.mean(u, axis=-1, keepdims=True)
        cen = u - mu
        var = jnp.mean(cen * cen, axis=-1, keepdims=True)
        o_ref[rows, :] = cen * lax.rsqrt(var + LN_EPS) * gain_ref[...] + bias_ref[...]

    n_sub = o_ref.shape[0] // sub
    rows = [slice(r * sub, (r + 1) * sub) for r in range(n_sub)]
    pending = matmuls(rows[0])
    for r in range(n_sub):
        following = matmuls(rows[r + 1]) if r + 1 < n_sub else None
        tail(rows[r], *pending)
        pending = following


def _output_stage(x2d, attn2d, ret2d, p2d, w_out, w_gate, w_proj, ln_gain, ln_bias, *, tm=1024, sub=256):
    m, dm = x2d.shape
    rows = lambda width: pl.BlockSpec((tm, width), lambda i: (i, 0))
    return pl.pallas_call(
        functools.partial(_output_kernel, sub=sub),
        grid=(m // tm,),
        in_specs=[rows(dm), rows(ATTN_WIDTH), rows(RET_WIDTH), rows(D_PLE),
                  _resident(w_out.shape), _resident(w_gate.shape), _resident(w_proj.shape),
                  _resident((1, dm)), _resident((1, dm))],
        out_specs=rows(dm),
        out_shape=jax.ShapeDtypeStruct((m, dm), F32),
        scratch_shapes=[pltpu.VMEM(w_out.shape, BF16), pltpu.VMEM(w_gate.shape, BF16),
                        pltpu.VMEM(w_proj.shape, BF16)],
        compiler_params=pltpu.CompilerParams(dimension_semantics=("arbitrary",),
                                             vmem_limit_bytes=VMEM_LIMIT),
        name="output_stage",
    )(x2d, attn2d, ret2d, p2d, w_out, w_gate, w_proj,
      ln_gain.astype(F32).reshape(1, dm), ln_bias.astype(F32).reshape(1, dm))


def kernel(x, p, w_in, w_out, ret_gn_gain, w_ple_gate, w_ple_proj, ln_gain, ln_bias):
    b, s, dm = x.shape
    h = x
    for i in range(DEPTH):
        h2d = h.reshape(b * s, dm)
        z = _input_projection(h2d, w_in[i].astype(F32), s)
        z3d = z.reshape(b, s, IN_WIDTH)
        attn = _moba_attention(z3d)
        ret = _retention(z3d, ret_gn_gain[i])
        out = _output_stage(h2d, attn.reshape(b * s, ATTN_WIDTH), ret.reshape(b * s, RET_WIDTH),
                            p[i].reshape(b * s, D_PLE), w_out[i].astype(F32), w_ple_gate[i].astype(F32),
                            w_ple_proj[i].astype(F32), ln_gain[i], ln_bias[i])
        h = out.reshape(b, s, dm)
    return h
```

```python
import functools

import jax
import jax.numpy as jnp
import numpy as np
from jax import lax
from jax.experimental import pallas as pl
from jax.experimental.pallas import tpu as pltpu

F32 = jnp.float32
BF16 = jnp.bfloat16

D_MODEL = 1024
D_PLE = 256
ATTN_HEADS = 8
ATTN_HEAD_DIM = 64
ATTN_WIDTH = ATTN_HEADS * ATTN_HEAD_DIM
RET_HEADS = 4
RET_HEAD_DIM = 128
RET_WIDTH = RET_HEADS * RET_HEAD_DIM
IN_WIDTH = 4 * ATTN_WIDTH + 4 * RET_WIDTH
MOBA_BLOCK = 256
MOBA_TOPK = 3
ROPE_BASE = 10000.0
DEPTH = 1
DEEPNORM_ALPHA = (2.0 * DEPTH) ** 0.25
LN_EPS = 1e-5
GN_EPS = 1e-6

LANES = 128
HEADS_PER_STEP = LANES // ATTN_HEAD_DIM
MASKED = -1e30
LOG2_E = 1.4426950408889634
ONES_ROWS = 16
CHUNK_BLOCKS = 4
PAIR_UNROLL = 2
RET_CHUNK = 256
VMEM_LIMIT = 48 * 1024 * 1024


def _proj_kernel(x_ref, w32_ref, cos_ref, sin_ref, o_ref, w_ref, *, n_chunk):
    @pl.when(pl.program_id(0) == 0)
    def _cast_weights_once():
        for lo in range(0, w_ref.shape[1], n_chunk):
            w_ref[:, lo:lo + n_chunk] = w32_ref[:, lo:lo + n_chunk].astype(BF16)

    xb = x_ref[...].astype(BF16)
    d = RET_HEAD_DIM
    rot_lo = 4 * ATTN_WIDTH
    rot_mid = rot_lo + RET_WIDTH
    rot_hi = rot_mid + RET_WIDTH
    assert rot_lo % n_chunk == 0 and rot_hi % n_chunk == 0 and n_chunk % d == 0
    for lo in range(0, o_ref.shape[1], n_chunk):
        acc = jnp.dot(xb, w_ref[:, lo:lo + n_chunk], preferred_element_type=F32)
        if rot_lo <= lo < rot_hi:
            cos = cos_ref[...]
            sin = sin_ref[...]
            for col in range(lo, lo + n_chunk, d):
                head = acc[:, col - lo:col - lo + d]
                rotated = head * cos + pltpu.roll(head, d // 2, 1) * sin
                if col >= rot_mid:
                    rotated = rotated * (d ** -0.5)
                o_ref[:, col:col + d] = rotated.astype(o_ref.dtype)
        else:
            o_ref[:, lo:lo + n_chunk] = acc.astype(o_ref.dtype)


def _rotary_tables(s):
    half = RET_HEAD_DIM // 2
    freqs = ROPE_BASE ** (-np.arange(half, dtype=np.float64) / half)
    ang = np.arange(s, dtype=np.float64)[:, None] * freqs[None, :]
    cos, sin = np.cos(ang), np.sin(ang)
    return (np.concatenate([cos, cos], axis=-1).astype(np.float32),
            np.concatenate([-sin, sin], axis=-1).astype(np.float32))


def _resident(shape):
    return pl.BlockSpec(shape, lambda *_: (0,) * len(shape), pipeline_mode=pl.Buffered(1))


def _input_projection(x2d, w_in, seq, *, tm=512, n_chunk=1024):
    m, k = x2d.shape
    n = w_in.shape[1]
    cos, sin = _rotary_tables(seq)
    pos = pl.BlockSpec((tm, RET_HEAD_DIM), lambda i: (i % (seq // tm), 0))
    return pl.pallas_call(
        functools.partial(_proj_kernel, n_chunk=n_chunk),
        grid=(m // tm,),
        in_specs=[pl.BlockSpec((tm, k), lambda i: (i, 0)), _resident((k, n)), pos, pos],
        out_specs=pl.BlockSpec((tm, n), lambda i: (i, 0)),
        out_shape=jax.ShapeDtypeStruct((m, n), BF16),
        scratch_shapes=[pltpu.VMEM((k, n), BF16)],
        compiler_params=pltpu.CompilerParams(dimension_semantics=("arbitrary",),
                                             vmem_limit_bytes=VMEM_LIMIT),
        name="input_projection",
    )(x2d, w_in, cos, sin)


def _moba_kernel(q_ref, k_ref, v_ref, g_ref, onehot_ref, o_ref,
                 vtc_ref, qtb_ref, s0_ref, s1_ref, mrun_ref, mnow_ref, alpha_ref, acc_ref, *, nb):
    blk = MOBA_BLOCK
    hd = ATTN_HEAD_DIM
    nh = HEADS_PER_STEP
    width = nh * blk
    vrows = LANES + ONES_ROWS
    seq = nb * blk
    chunk = CHUNK_BLOCKS * blk
    n_chunks = nb // CHUNK_BLOCKS
    s_refs = (s0_ref, s1_ref)

    @pl.when((pl.program_id(0) == 0) & (pl.program_id(1) == 0))
    def _fill_constant_regions():
        vtc_ref[:, LANES:vrows, :] = jnp.ones((n_chunks, ONES_ROWS, chunk), BF16)
        qtb_ref[...] = jnp.zeros(qtb_ref.shape, BF16)

    def _prepare_head_pair():
        for n in range(nb):
            c, j = divmod(n, CHUNK_BLOCKS)
            vtc_ref[c, 0:LANES, j * blk:(j + 1) * blk] = (
                v_ref[0, n * blk:(n + 1) * blk, :].astype(F32).T.astype(BF16))
        key_blk = lax.broadcasted_iota(jnp.int32, (nb, seq), 1) // blk
        row_blk = lax.broadcasted_iota(jnp.int32, (nb, seq), 0)
        averager = jnp.where(key_blk == row_blk, 1.0 / blk, 0.0).astype(BF16)
        kmean = jnp.dot(averager, k_ref[0], preferred_element_type=F32)
        hi = kmean.astype(BF16)
        lo = (kmean - hi.astype(F32)).astype(BF16)
        lane = lax.broadcasted_iota(jnp.int32, (nb, LANES), 1)
        parts = []
        for h in range(nh):
            in_head = (lane >= h * hd) & (lane < (h + 1) * hd)
            parts += [jnp.where(in_head, hi, jnp.zeros_like(hi)), jnp.where(in_head, lo, jnp.zeros_like(lo))]
        kstack = jnp.concatenate(parts, axis=0)
        qts = []
        for n in range(nb):
            qt = (q_ref[0, n * blk:(n + 1) * blk, :].astype(F32) * (hd ** -0.5 * LOG2_E)).T.astype(BF16)
            for h in range(nh):
                qtb_ref[n, h * hd:(h + 1) * hd, h * blk:(h + 1) * blk] = qt[h * hd:(h + 1) * hd]
            qts.append(qt)
        gates = jnp.dot(kstack, jnp.concatenate(qts, axis=1), preferred_element_type=F32)
        past = row_blk < key_blk
        row_f = row_blk.astype(F32)
        own_f = jnp.where(row_blk == key_blk, 1.0, 0.0)
        for h in range(nh):
            g = gates[(2 * h) * nb:(2 * h + 1) * nb] + gates[(2 * h + 1) * nb:(2 * h + 2) * nb]
            g = jnp.where(past, g, -jnp.inf)
            picked = jnp.zeros((nb, seq), F32)
            for _ in range(MOBA_TOPK):
                top = jnp.max(g, axis=0, keepdims=True)
                first = jnp.min(jnp.where(g == top, row_f, float(nb)), axis=0, keepdims=True)
                pick = row_f == first
                picked = jnp.where(pick, 1.0, picked)
                g = jnp.where(pick, -jnp.inf, g)
            bias = jnp.where(jnp.where(past, picked, own_f) > 0.5, 0.0, MASKED).astype(BF16)
            for n in range(nb):
                qtb_ref[n, LANES:LANES + nb, h * blk:(h + 1) * blk] = bias[:, n * blk:(n + 1) * blk]

    _prepare_head_pair()

    key_pos = lax.broadcasted_iota(jnp.int32, (blk, width), 0)
    q_pos = lax.broadcasted_iota(jnp.int32, (blk, width), 1) & (blk - 1)
    causal = key_pos <= q_pos

    def rows_at(start, size):
        if isinstance(start, int):
            return pl.ds(start, size)
        return pl.ds(pl.multiple_of(start, blk), size)

    def scores(c, qi, slot, cnt, own_j=None):
        key_rows = rows_at(c * chunk, cnt * blk)
        keys_ext = jnp.concatenate([k_ref[0, key_rows, :], onehot_ref[key_rows, :]], axis=1)
        s = jnp.dot(keys_ext, qtb_ref[qi], preferred_element_type=F32)
        m8 = None
        for j in range(cnt):
            sj = s[j * blk:(j + 1) * blk]
            if j == own_j:
                sj = jnp.where(causal, sj, MASKED)
            s_refs[slot][j * blk:(j + 1) * blk, :] = sj
            mj = jnp.max(sj.reshape(blk // 8, 8, width), axis=0)
            m8 = mj if m8 is None else jnp.maximum(m8, mj)
        m_new = jnp.broadcast_to(jnp.max(m8, axis=0, keepdims=True), (8, width))
        if c > 0:
            m_old = mrun_ref[qi]
            m_new = jnp.maximum(m_old, m_new)
            alpha_ref[slot] = jnp.exp2(m_old - m_new)
        mrun_ref[qi] = m_new
        mnow_ref[slot] = m_new

    def values(c, qi, slot, cnt):
        p = jnp.exp2(s_refs[slot][0:cnt * blk, :] - mnow_ref[slot, 0:1, :]).astype(BF16)
        pv = jnp.dot(vtc_ref[c, :, 0:cnt * blk], p, preferred_element_type=F32)
        if c > 0:
            pv = acc_ref[qi] * alpha_ref[slot, 0:1, :] + pv
        acc_ref[qi] = pv

    def finish(qi):
        outs = []
        for h in range(nh):
            cols = slice(h * blk, (h + 1) * blk)
            outs.append(acc_ref[qi, h * hd:(h + 1) * hd, cols] / acc_ref[qi, LANES:LANES + 1, cols])
        attn = jnp.concatenate(outs, axis=0).T
        gate = g_ref[0, rows_at(qi * blk, blk), :].astype(F32)
        o_ref[0, rows_at(qi * blk, blk), :] = (attn * (gate * jax.nn.sigmoid(gate))).astype(o_ref.dtype)

    assert CHUNK_BLOCKS % 2 == 0 and nb % CHUNK_BLOCKS == 0
    for c in range(n_chunks):
        first = c * CHUNK_BLOCKS
        later = first + CHUNK_BLOCKS
        n_pairs = (nb - later) // 2
        scores(c, first, 0, 1, own_j=0)
        for j in range(CHUNK_BLOCKS):
            if j + 1 < CHUNK_BLOCKS:
                scores(c, first + j + 1, (j + 1) % 2, j + 2, own_j=j + 1)
            elif n_pairs > 0:
                scores(c, later, 0, CHUNK_BLOCKS)
            values(c, first + j, j % 2, j + 1)
            finish(first + j)
        if n_pairs == 0:
            continue

        group = 2 * PAIR_UNROLL
        assert (nb - later) % group == 0

        def query_group(start, last):
            for i in range(group):
                if not (last and i == group - 1):
                    scores(c, start + i + 1, (i + 1) % 2, CHUNK_BLOCKS)
                values(c, start + i, i % 2, CHUNK_BLOCKS)

        def loop_body(t, carry, later=later):
            query_group(later + group * t, last=False)
            return carry

        n_groups = (nb - later) // group
        lax.fori_loop(0, n_groups - 1, loop_body, 0)
        query_group(nb - group, last=True)


def _moba_attention(z3d):
    b, s, _ = z3d.shape
    nb = s // MOBA_BLOCK
    steps = ATTN_WIDTH // LANES
    width = HEADS_PER_STEP * MOBA_BLOCK
    vrows = LANES + ONES_ROWS
    kernel = functools.partial(_moba_kernel, nb=nb)
    full = lambda off: pl.BlockSpec((1, s, LANES), lambda bi, hp: (bi, 0, off * steps + hp))
    onehot = jnp.asarray(np.arange(s)[:, None] // MOBA_BLOCK == np.arange(LANES)[None, :], dtype=BF16)
    return pl.pallas_call(
        kernel,
        grid=(b, steps),
        in_specs=[full(0), full(1), full(2), full(3), _resident((s, LANES))],
        out_specs=pl.BlockSpec((1, s, LANES), lambda bi, hp: (bi, 0, hp)),
        out_shape=jax.ShapeDtypeStruct((b, s, ATTN_WIDTH), BF16),
        scratch_shapes=[pltpu.VMEM((nb // CHUNK_BLOCKS, vrows, CHUNK_BLOCKS * MOBA_BLOCK), BF16),
                        pltpu.VMEM((nb, 2 * LANES, width), BF16),
                        pltpu.VMEM((CHUNK_BLOCKS * MOBA_BLOCK, width), F32),
                        pltpu.VMEM((CHUNK_BLOCKS * MOBA_BLOCK, width), F32),
                        pltpu.VMEM((nb, 8, width), F32),
                        pltpu.VMEM((2, 8, width), F32),
                        pltpu.VMEM((2, 8, width), F32),
                        pltpu.VMEM((nb, vrows, width), F32)],
        compiler_params=pltpu.CompilerParams(dimension_semantics=("arbitrary",) * 2,
                                             vmem_limit_bytes=VMEM_LIMIT),
        name="moba_attention",
    )(z3d, z3d, z3d, z3d, onehot)


def _retention_kernel(q_ref, k_ref, v_ref, g_ref, decay_ref, xi_ref, zeta_ref, gc_ref, gain_ref,
                      o_ref, state_ref):
    c = pl.program_id(0)
    d = RET_HEAD_DIM

    @pl.when(c == 0)
    def _reset_state():
        state_ref[...] = jnp.zeros_like(state_ref)

    for bi in range(q_ref.shape[0]):
        for h in range(RET_HEADS):
            cols = slice(h * d, (h + 1) * d)
            qb = q_ref[bi, :, cols]
            kb = k_ref[bi, :, cols]
            vb = v_ref[bi, :, cols]
            scores = lax.dot_general(qb, kb, (((1,), (1,)), ((), ())), preferred_element_type=F32)
            inner = (scores * decay_ref[h]).astype(BF16)
            state = state_ref[bi, h]
            out = (jnp.dot(inner, vb, preferred_element_type=F32)
                   + jnp.dot(qb, state.astype(BF16), preferred_element_type=F32) * xi_ref[h])
            vz = (vb.astype(F32) * zeta_ref[h]).astype(BF16)
            state_ref[bi, h] = state * gc_ref[h] + lax.dot_general(
                kb, vz, (((0,), (0,)), ((), ())), preferred_element_type=F32)

            mu = jnp.mean(out, axis=-1, keepdims=True)
            cen = out - mu
            var = jnp.mean(cen * cen, axis=-1, keepdims=True)
            normed = cen * lax.rsqrt(var + GN_EPS) * gain_ref[h]
            gate = g_ref[bi, :, cols].astype(F32)
            o_ref[bi, :, cols] = (normed * (gate * jax.nn.sigmoid(gate))).astype(o_ref.dtype)


def _retention_constants():
    d, c = RET_HEAD_DIM, RET_CHUNK
    log_g = np.log1p(-np.exp2(-5.0 - np.arange(RET_HEADS, dtype=np.float64)))
    idx = np.arange(c, dtype=np.float64)
    rel = idx[:, None] - idx[None, :]
    decay = np.where(rel >= 0, np.exp(np.maximum(rel, 0.0)[None] * log_g[:, None, None]), 0.0)
    xi = np.exp((idx + 1.0)[None] * log_g[:, None])
    zeta = np.exp((c - 1.0 - idx)[None] * log_g[:, None])
    g_chunk = np.exp(c * log_g)
    bcast = lambda t: np.ascontiguousarray(np.broadcast_to(t[..., None], t.shape + (d,)), dtype=np.float32)
    return decay.astype(np.float32), bcast(xi), bcast(zeta), bcast(g_chunk[:, None])


def _retention(z3d, ret_gn_gain):
    b, s, _ = z3d.shape
    c, d = RET_CHUNK, RET_HEAD_DIM
    base = 4 * ATTN_WIDTH // RET_WIDTH
    decay, xi, zeta, gc = _retention_constants()
    gain = ret_gn_gain.astype(F32).reshape(RET_HEADS, 1, d)
    zspec = lambda off: pl.BlockSpec((b, c, RET_WIDTH), lambda ci: (0, ci, base + off))
    whole = lambda *shape: pl.BlockSpec(shape, lambda ci: (0,) * len(shape))
    return pl.pallas_call(
        _retention_kernel,
        grid=(s // c,),
        in_specs=[zspec(0), zspec(1), zspec(2), zspec(3),
                  whole(RET_HEADS, c, c), whole(RET_HEADS, c, d), whole(RET_HEADS, c, d),
                  whole(RET_HEADS, 1, d), whole(RET_HEADS, 1, d)],
        out_specs=pl.BlockSpec((b, c, RET_WIDTH), lambda ci: (0, ci, 0)),
        out_shape=jax.ShapeDtypeStruct((b, s, RET_WIDTH), BF16),
        scratch_shapes=[pltpu.VMEM((b, RET_HEADS, d, d), F32)],
        compiler_params=pltpu.CompilerParams(dimension_semantics=("arbitrary",),
                                             vmem_limit_bytes=VMEM_LIMIT),
        name="retention",
    )(z3d, z3d, z3d, z3d, decay, xi, zeta, gc, gain)


def _output_kernel(x_ref, a_ref, r_ref, p_ref, wo32_ref, wg32_ref, wp32_ref, gain_ref, bias_ref, o_ref,
                   wo_ref, wg_ref, wp_ref, *, sub):
    @pl.when(pl.program_id(0) == 0)
    def _cast_weights_once():
        wo_ref[...] = wo32_ref[...].astype(BF16)
        wg_ref[...] = wg32_ref[...].astype(BF16)
        wp_ref[...] = wp32_ref[...].astype(BF16)

    def matmuls(rows):
        mix = (jnp.dot(a_ref[rows, :], wo_ref[0:ATTN_WIDTH, :], preferred_element_type=F32)
               + jnp.dot(r_ref[rows, :], wo_ref[ATTN_WIDTH:, :], preferred_element_type=F32))
        u = DEEPNORM_ALPHA * x_ref[rows, :] + mix
        ple = jnp.dot(p_ref[rows, :].astype(BF16), wp_ref[...], preferred_element_type=F32)
        gate_in = jnp.dot(u.astype(BF16), wg_ref[...], preferred_element_type=F32)
        return u, ple, gate_in

    def tail(rows, u, ple, gate_in):
        u = u + jax.nn.sigmoid(gate_in) * ple
        mu = jnp.mean(u, axis=-1, keepdims=True)
        cen = u - mu
        var = jnp.mean(cen * cen, axis=-1, keepdims=True)
        o_ref[rows, :] = cen * lax.rsqrt(var + LN_EPS) * gain_ref[...] + bias_ref[...]

    n_sub = o_ref.shape[0] // sub
    rows = [slice(r * sub, (r + 1) * sub) for r in range(n_sub)]
    pending = matmuls(rows[0])
    for r in range(n_sub):
        following = matmuls(rows[r + 1]) if r + 1 < n_sub else None
        tail(rows[r], *pending)
        pending = following


def _output_stage(x2d, attn2d, ret2d, p2d, w_out, w_gate, w_proj, ln_gain, ln_bias, *, tm=1024, sub=256):
    m, dm = x2d.shape
    rows = lambda width: pl.BlockSpec((tm, width), lambda i: (i, 0))
    return pl.pallas_call(
        functools.partial(_output_kernel, sub=sub),
        grid=(m // tm,),
        in_specs=[rows(dm), rows(ATTN_WIDTH), rows(RET_WIDTH), rows(D_PLE),
                  _resident(w_out.shape), _resident(w_gate.shape), _resident(w_proj.shape),
                  _resident((1, dm)), _resident((1, dm))],
        out_specs=rows(dm),
        out_shape=jax.ShapeDtypeStruct((m, dm), F32),
        scratch_shapes=[pltpu.VMEM(w_out.shape, BF16), pltpu.VMEM(w_gate.shape, BF16),
                        pltpu.VMEM(w_proj.shape, BF16)],
        compiler_params=pltpu.CompilerParams(dimension_semantics=("arbitrary",),
                                             vmem_limit_bytes=VMEM_LIMIT),
        name="output_stage",
    )(x2d, attn2d, ret2d, p2d, w_out, w_gate, w_proj,
      ln_gain.astype(F32).reshape(1, dm), ln_bias.astype(F32).reshape(1, dm))


def kernel(x, p, w_in, w_out, ret_gn_gain, w_ple_gate, w_ple_proj, ln_gain, ln_bias):
    b, s, dm = x.shape
    h = x
    for i in range(DEPTH):
        h2d = h.reshape(b * s, dm)
        z = _input_projection(h2d, w_in[i].astype(F32), s)
        z3d = z.reshape(b, s, IN_WIDTH)
        attn = _moba_attention(z3d)
        ret = _retention(z3d, ret_gn_gain[i])
        out = _output_stage(h2d, attn.reshape(b * s, ATTN_WIDTH), ret.reshape(b * s, RET_WIDTH),
                            p[i].reshape(b * s, D_PLE), w_out[i].astype(F32), w_ple_gate[i].astype(F32),
                            w_ple_proj[i].astype(F32), ln_gain[i], ln_bias[i])
        h = out.reshape(b, s, dm)
    return h
```

```python
import functools

import jax
import jax.numpy as jnp
import numpy as np
from jax import lax
from jax.experimental import pallas as pl
from jax.experimental.pallas import tpu as pltpu

F32 = jnp.float32
BF16 = jnp.bfloat16

D_MODEL = 1024
D_PLE = 256
ATTN_HEADS = 8
ATTN_HEAD_DIM = 64
ATTN_WIDTH = ATTN_HEADS * ATTN_HEAD_DIM
RET_HEADS = 4
RET_HEAD_DIM = 128
RET_WIDTH = RET_HEADS * RET_HEAD_DIM
IN_WIDTH = 4 * ATTN_WIDTH + 4 * RET_WIDTH
MOBA_BLOCK = 256
MOBA_TOPK = 3
ROPE_BASE = 10000.0
DEPTH = 1
DEEPNORM_ALPHA = (2.0 * DEPTH) ** 0.25
LN_EPS = 1e-5
GN_EPS = 1e-6

LANES = 128
HEADS_PER_STEP = LANES // ATTN_HEAD_DIM
MASKED = -1e30
LOG2_E = 1.4426950408889634
ONES_ROWS = 16
CHUNK_BLOCKS = 4
PAIR_UNROLL = 2
RET_CHUNK = 256
VMEM_LIMIT = 48 * 1024 * 1024


def _proj_kernel(x_ref, w32_ref, cos_ref, sin_ref, o_ref, w_ref, *, n_chunk):
    @pl.when(pl.program_id(0) == 0)
    def _cast_weights_once():
        for lo in range(0, w_ref.shape[1], n_chunk):
            w_ref[:, lo:lo + n_chunk] = w32_ref[:, lo:lo + n_chunk].astype(BF16)

    xb = x_ref[...].astype(BF16)
    d = RET_HEAD_DIM
    rot_lo = 4 * ATTN_WIDTH
    rot_mid = rot_lo + RET_WIDTH
    rot_hi = rot_mid + RET_WIDTH
    assert rot_lo % n_chunk == 0 and rot_hi % n_chunk == 0 and n_chunk % d == 0
    for lo in range(0, o_ref.shape[1], n_chunk):
        acc = jnp.dot(xb, w_ref[:, lo:lo + n_chunk], preferred_element_type=F32)
        if rot_lo <= lo < rot_hi:
            cos = cos_ref[...]
            sin = sin_ref[...]
            for col in range(lo, lo + n_chunk, d):
                head = acc[:, col - lo:col - lo + d]
                rotated = head * cos + pltpu.roll(head, d // 2, 1) * sin
                if col >= rot_mid:
                    rotated = rotated * (d ** -0.5)
                o_ref[:, col:col + d] = rotated.astype(o_ref.dtype)
        else:
            o_ref[:, lo:lo + n_chunk] = acc.astype(o_ref.dtype)


def _rotary_tables(s):
    half = RET_HEAD_DIM // 2
    freqs = ROPE_BASE ** (-np.arange(half, dtype=np.float64) / half)
    ang = np.arange(s, dtype=np.float64)[:, None] * freqs[None, :]
    cos, sin = np.cos(ang), np.sin(ang)
    return (np.concatenate([cos, cos], axis=-1).astype(np.float32),
            np.concatenate([-sin, sin], axis=-1).astype(np.float32))


def _resident(shape):
    return pl.BlockSpec(shape, lambda *_: (0,) * len(shape), pipeline_mode=pl.Buffered(1))


def _input_projection(x2d, w_in, seq, *, tm=512, n_chunk=1024):
    m, k = x2d.shape
    n = w_in.shape[1]
    cos, sin = _rotary_tables(seq)
    pos = pl.BlockSpec((tm, RET_HEAD_DIM), lambda i: (i % (seq // tm), 0))
    return pl.pallas_call(
        functools.partial(_proj_kernel, n_chunk=n_chunk),
        grid=(m // tm,),
        in_specs=[pl.BlockSpec((tm, k), lambda i: (i, 0)), _resident((k, n)), pos, pos],
        out_specs=pl.BlockSpec((tm, n), lambda i: (i, 0)),
        out_shape=jax.ShapeDtypeStruct((m, n), BF16),
        scratch_shapes=[pltpu.VMEM((k, n), BF16)],
        compiler_params=pltpu.CompilerParams(dimension_semantics=("arbitrary",),
                                             vmem_limit_bytes=VMEM_LIMIT),
        name="input_projection",
    )(x2d, w_in, cos, sin)


def _moba_kernel(q_ref, k_ref, v_ref, g_ref, onehot_ref, o_ref,
                 vtc_ref, qtb_ref, s0_ref, s1_ref, mrun_ref, mnow_ref, alpha_ref, acc_ref, *, nb):
    blk = MOBA_BLOCK
    hd = ATTN_HEAD_DIM
    nh = HEADS_PER_STEP
    width = nh * blk
    vrows = LANES + ONES_ROWS
    seq = nb * blk
    chunk = CHUNK_BLOCKS * blk
    n_chunks = nb // CHUNK_BLOCKS
    s_refs = (s0_ref, s1_ref)

    @pl.when((pl.program_id(0) == 0) & (pl.program_id(1) == 0))
    def _fill_constant_regions():
        vtc_ref[:, LANES:vrows, :] = jnp.ones((n_chunks, ONES_ROWS, chunk), BF16)
        qtb_ref[...] = jnp.zeros(qtb_ref.shape, BF16)

    def _prepare_head_pair():
        for n in range(nb):
            c, j = divmod(n, CHUNK_BLOCKS)
            vtc_ref[c, 0:LANES, j * blk:(j + 1) * blk] = (
                v_ref[0, n * blk:(n + 1) * blk, :].astype(F32).T.astype(BF16))
        key_blk = lax.broadcasted_iota(jnp.int32, (nb, seq), 1) // blk
        row_blk = lax.broadcasted_iota(jnp.int32, (nb, seq), 0)
        averager = jnp.where(key_blk == row_blk, 1.0 / blk, 0.0).astype(BF16)
        kmean = jnp.dot(averager, k_ref[0], preferred_element_type=F32)
        hi = kmean.astype(BF16)
        lo = (kmean - hi.astype(F32)).astype(BF16)
        lane = lax.broadcasted_iota(jnp.int32, (nb, LANES), 1)
        parts = []
        for h in range(nh):
            in_head = (lane >= h * hd) & (lane < (h + 1) * hd)
            parts += [jnp.where(in_head, hi, jnp.zeros_like(hi)), jnp.where(in_head, lo, jnp.zeros_like(lo))]
        kstack = jnp.concatenate(parts, axis=0)
        qts = []
        for n in range(nb):
            qt = (q_ref[0, n * blk:(n + 1) * blk, :].astype(F32) * (hd ** -0.5 * LOG2_E)).T.astype(BF16)
            for h in range(nh):
                qtb_ref[n, h * hd:(h + 1) * hd, h * blk:(h + 1) * blk] = qt[h * hd:(h + 1) * hd]
            qts.append(qt)
        gates = jnp.dot(kstack, jnp.concatenate(qts, axis=1), preferred_element_type=F32)
        past = row_blk < key_blk
        row_f = row_blk.astype(F32)
        own_f = jnp.where(row_blk == key_blk, 1.0, 0.0)
        for h in range(nh):
            g = gates[(2 * h) * nb:(2 * h + 1) * nb] + gates[(2 * h + 1) * nb:(2 * h + 2) * nb]
            g = jnp.where(past, g, -jnp.inf)
            picked = jnp.zeros((nb, seq), F32)
            for _ in range(MOBA_TOPK):
                top = jnp.max(g, axis=0, keepdims=True)
                first = jnp.min(jnp.where(g == top, row_f, float(nb)), axis=0, keepdims=True)
                pick = row_f == first
                picked = jnp.where(pick, 1.0, picked)
                g = jnp.where(pick, -jnp.inf, g)
            bias = jnp.where(jnp.where(past, picked, own_f) > 0.5, 0.0, MASKED).astype(BF16)
            for n in range(nb):
                qtb_ref[n, LANES:LANES + nb, h * blk:(h + 1) * blk] = bias[:, n * blk:(n + 1) * blk]

    _prepare_head_pair()

    key_pos = lax.broadcasted_iota(jnp.int32, (blk, width), 0)
    q_pos = lax.broadcasted_iota(jnp.int32, (blk, width), 1) & (blk - 1)
    causal = key_pos <= q_pos

    def rows_at(start, size):
        if isinstance(start, int):
            return pl.ds(start, size)
        return pl.ds(pl.multiple_of(start, blk), size)

    def scores(c, qi, slot, cnt, own_j=None):
        key_rows = rows_at(c * chunk, cnt * blk)
        keys_ext = jnp.concatenate([k_ref[0, key_rows, :], onehot_ref[key_rows, :]], axis=1)
        s = jnp.dot(keys_ext, qtb_ref[qi], preferred_element_type=F32)
        m8 = None
        for j in range(cnt):
            sj = s[j * blk:(j + 1) * blk]
            if j == own_j:
                sj = jnp.where(causal, sj, MASKED)
            s_refs[slot][j * blk:(j + 1) * blk, :] = sj
            mj = jnp.max(sj.reshape(blk // 8, 8, width), axis=0)
            m8 = mj if m8 is None else jnp.maximum(m8, mj)
        m_new = jnp.broadcast_to(jnp.max(m8, axis=0, keepdims=True), (8, width))
        if c > 0:
            m_old = mrun_ref[qi]
            m_new = jnp.maximum(m_old, m_new)
            alpha_ref[slot] = jnp.exp2(m_old - m_new)
        mrun_ref[qi] = m_new
        mnow_ref[slot] = m_new

    def values(c, qi, slot, cnt):
        p = jnp.exp2(s_refs[slot][0:cnt * blk, :] - mnow_ref[slot, 0:1, :]).astype(BF16)
        pv = jnp.dot(vtc_ref[c, :, 0:cnt * blk], p, preferred_element_type=F32)
        if c > 0:
            pv = acc_ref[qi] * alpha_ref[slot, 0:1, :] + pv
        acc_ref[qi] = pv

    def finish(qi):
        outs = []
        for h in range(nh):
            cols = slice(h * blk, (h + 1) * blk)
            outs.append(acc_ref[qi, h * hd:(h + 1) * hd, cols] / acc_ref[qi, LANES:LANES + 1, cols])
        attn = jnp.concatenate(outs, axis=0).T
        gate = g_ref[0, rows_at(qi * blk, blk), :].astype(F32)
        o_ref[0, rows_at(qi * blk, blk), :] = (attn * (gate * jax.nn.sigmoid(gate))).astype(o_ref.dtype)

    assert CHUNK_BLOCKS % 2 == 0 and nb % CHUNK_BLOCKS == 0
    def chunk_section(c, unroll):
        first = c * CHUNK_BLOCKS
        later = first + CHUNK_BLOCKS
        n_pairs = (nb - later) // 2
        scores(c, first, 0, 1, own_j=0)
        for j in range(CHUNK_BLOCKS):
            if j + 1 < CHUNK_BLOCKS:
                scores(c, first + j + 1, (j + 1) % 2, j + 2, own_j=j + 1)
            elif n_pairs > 0:
                scores(c, later, 0, CHUNK_BLOCKS)
            values(c, first + j, j % 2, j + 1)
            finish(first + j)
        if n_pairs == 0:
            return

        group = 2 * PAIR_UNROLL
        assert (nb - later) % group == 0

        def query_group(start, last):
            for i in range(group):
                if not (last and i == group - 1):
                    scores(c, start + i + 1, (i + 1) % 2, CHUNK_BLOCKS)
                values(c, start + i, i % 2, CHUNK_BLOCKS)

        def loop_body(t, carry):
            query_group(later + group * t, last=False)
            return carry

        n_groups = (nb - later) // group
        if unroll:
            for t in range(n_groups - 1):
                query_group(later + group * t, last=False)
        else:
            lax.fori_loop(0, n_groups - 1, loop_body, 0)
        query_group(nb - group, last=True)

    chunk_section(0, unroll=True)

    @pl.when(pl.program_id(1) >= 0)
    def _remaining_chunks():
        for c in range(1, n_chunks):
            chunk_section(c, unroll=False)


def _moba_attention(z3d):
    b, s, _ = z3d.shape
    nb = s // MOBA_BLOCK
    steps = ATTN_WIDTH // LANES
    width = HEADS_PER_STEP * MOBA_BLOCK
    vrows = LANES + ONES_ROWS
    kernel = functools.partial(_moba_kernel, nb=nb)
    full = lambda off: pl.BlockSpec((1, s, LANES), lambda bi, hp: (bi, 0, off * steps + hp))
    onehot = jnp.asarray(np.arange(s)[:, None] // MOBA_BLOCK == np.arange(LANES)[None, :], dtype=BF16)
    return pl.pallas_call(
        kernel,
        grid=(b, steps),
        in_specs=[full(0), full(1), full(2), full(3), _resident((s, LANES))],
        out_specs=pl.BlockSpec((1, s, LANES), lambda bi, hp: (bi, 0, hp)),
        out_shape=jax.ShapeDtypeStruct((b, s, ATTN_WIDTH), BF16),
        scratch_shapes=[pltpu.VMEM((nb // CHUNK_BLOCKS, vrows, CHUNK_BLOCKS * MOBA_BLOCK), BF16),
                        pltpu.VMEM((nb, 2 * LANES, width), BF16),
                        pltpu.VMEM((CHUNK_BLOCKS * MOBA_BLOCK, width), F32),
                        pltpu.VMEM((CHUNK_BLOCKS * MOBA_BLOCK, width), F32),
                        pltpu.VMEM((nb, 8, width), F32),
                        pltpu.VMEM((2, 8, width), F32),
                        pltpu.VMEM((2, 8, width), F32),
                        pltpu.VMEM((nb, vrows, width), F32)],
        compiler_params=pltpu.CompilerParams(dimension_semantics=("arbitrary",) * 2,
                                             vmem_limit_bytes=VMEM_LIMIT),
        name="moba_attention",
    )(z3d, z3d, z3d, z3d, onehot)


def _retention_kernel(q_ref, k_ref, v_ref, g_ref, decay_ref, xi_ref, zeta_ref, gc_ref, gain_ref,
                      o_ref, state_ref):
    c = pl.program_id(0)
    d = RET_HEAD_DIM

    @pl.when(c == 0)
    def _reset_state():
        state_ref[...] = jnp.zeros_like(state_ref)

    for bi in range(q_ref.shape[0]):
        for h in range(RET_HEADS):
            cols = slice(h * d, (h + 1) * d)
            qb = q_ref[bi, :, cols]
            kb = k_ref[bi, :, cols]
            vb = v_ref[bi, :, cols]
            scores = lax.dot_general(qb, kb, (((1,), (1,)), ((), ())), preferred_element_type=F32)
            inner = (scores * decay_ref[h]).astype(BF16)
            state = state_ref[bi, h]
            out = (jnp.dot(inner, vb, preferred_element_type=F32)
                   + jnp.dot(qb, state.astype(BF16), preferred_element_type=F32) * xi_ref[h])
            vz = (vb.astype(F32) * zeta_ref[h]).astype(BF16)
            state_ref[bi, h] = state * gc_ref[h] + lax.dot_general(
                kb, vz, (((0,), (0,)), ((), ())), preferred_element_type=F32)

            mu = jnp.mean(out, axis=-1, keepdims=True)
            cen = out - mu
            var = jnp.mean(cen * cen, axis=-1, keepdims=True)
            normed = cen * lax.rsqrt(var + GN_EPS) * gain_ref[h]
            gate = g_ref[bi, :, cols].astype(F32)
            o_ref[bi, :, cols] = (normed * (gate * jax.nn.sigmoid(gate))).astype(o_ref.dtype)


def _retention_constants():
    d, c = RET_HEAD_DIM, RET_CHUNK
    log_g = np.log1p(-np.exp2(-5.0 - np.arange(RET_HEADS, dtype=np.float64)))
    idx = np.arange(c, dtype=np.float64)
    rel = idx[:, None] - idx[None, :]
    decay = np.where(rel >= 0, np.exp(np.maximum(rel, 0.0)[None] * log_g[:, None, None]), 0.0)
    xi = np.exp((idx + 1.0)[None] * log_g[:, None])
    zeta = np.exp((c - 1.0 - idx)[None] * log_g[:, None])
    g_chunk = np.exp(c * log_g)
    bcast = lambda t: np.ascontiguousarray(np.broadcast_to(t[..., None], t.shape + (d,)), dtype=np.float32)
    return decay.astype(np.float32), bcast(xi), bcast(zeta), bcast(g_chunk[:, None])


def _retention(z3d, ret_gn_gain):
    b, s, _ = z3d.shape
    c, d = RET_CHUNK, RET_HEAD_DIM
    base = 4 * ATTN_WIDTH // RET_WIDTH
    decay, xi, zeta, gc = _retention_constants()
    gain = ret_gn_gain.astype(F32).reshape(RET_HEADS, 1, d)
    zspec = lambda off: pl.BlockSpec((b, c, RET_WIDTH), lambda ci: (0, ci, base + off))
    whole = lambda *shape: pl.BlockSpec(shape, lambda ci: (0,) * len(shape))
    return pl.pallas_call(
        _retention_kernel,
        grid=(s // c,),
        in_specs=[zspec(0), zspec(1), zspec(2), zspec(3),
                  whole(RET_HEADS, c, c), whole(RET_HEADS, c, d), whole(RET_HEADS, c, d),
                  whole(RET_HEADS, 1, d), whole(RET_HEADS, 1, d)],
        out_specs=pl.BlockSpec((b, c, RET_WIDTH), lambda ci: (0, ci, 0)),
        out_shape=jax.ShapeDtypeStruct((b, s, RET_WIDTH), BF16),
        scratch_shapes=[pltpu.VMEM((b, RET_HEADS, d, d), F32)],
        compiler_params=pltpu.CompilerParams(dimension_semantics=("arbitrary",),
                                             vmem_limit_bytes=VMEM_LIMIT),
        name="retention",
    )(z3d, z3d, z3d, z3d, decay, xi, zeta, gc, gain)


def _output_kernel(x_ref, a_ref, r_ref, p_ref, wo32_ref, wg32_ref, wp32_ref, gain_ref, bias_ref, o_ref,
                   wo_ref, wg_ref, wp_ref, *, sub):
    @pl.when(pl.program_id(0) == 0)
    def _cast_weights_once():
        wo_ref[...] = wo32_ref[...].astype(BF16)
        wg_ref[...] = wg32_ref[...].astype(BF16)
        wp_ref[...] = wp32_ref[...].astype(BF16)

    def matmuls(rows):
        mix = (jnp.dot(a_ref[rows, :], wo_ref[0:ATTN_WIDTH, :], preferred_element_type=F32)
               + jnp.dot(r_ref[rows, :], wo_ref[ATTN_WIDTH:, :], preferred_element_type=F32))
        u = DEEPNORM_ALPHA * x_ref[rows, :] + mix
        ple = jnp.dot(p_ref[rows, :].astype(BF16), wp_ref[...], preferred_element_type=F32)
        gate_in = jnp.dot(u.astype(BF16), wg_ref[...], preferred_element_type=F32)
        return u, ple, gate_in

    def tail(rows, u, ple, gate_in):
        u = u + jax.nn.sigmoid(gate_in) * ple
        mu = jnp.mean(u, axis=-1, keepdims=True)
        cen = u - mu
        var = jnp.mean(cen * cen, axis=-1, keepdims=True)
        o_ref[rows, :] = cen * lax.rsqrt(var + LN_EPS) * gain_ref[...] + bias_ref[...]

    n_sub = o_ref.shape[0] // sub
    rows = [slice(r * sub, (r + 1) * sub) for r in range(n_sub)]
    pending = matmuls(rows[0])
    for r in range(n_sub):
        following = matmuls(rows[r + 1]) if r + 1 < n_sub else None
        tail(rows[r], *pending)
        pending = following


def _output_stage(x2d, attn2d, ret2d, p2d, w_out, w_gate, w_proj, ln_gain, ln_bias, *, tm=1024, sub=256):
    m, dm = x2d.shape
    rows = lambda width: pl.BlockSpec((tm, width), lambda i: (i, 0))
    return pl.pallas_call(
        functools.partial(_output_kernel, sub=sub),
        grid=(m // tm,),
        in_specs=[rows(dm), rows(ATTN_WIDTH), rows(RET_WIDTH), rows(D_PLE),
                  _resident(w_out.shape), _resident(w_gate.shape), _resident(w_proj.shape),
                  _resident((1, dm)), _resident((1, dm))],
        out_specs=rows(dm),
        out_shape=jax.ShapeDtypeStruct((m, dm), F32),
        scratch_shapes=[pltpu.VMEM(w_out.shape, BF16), pltpu.VMEM(w_gate.shape, BF16),
                        pltpu.VMEM(w_proj.shape, BF16)],
        compiler_params=pltpu.CompilerParams(dimension_semantics=("arbitrary",),
                                             vmem_limit_bytes=VMEM_LIMIT),
        name="output_stage",
    )(x2d, attn2d, ret2d, p2d, w_out, w_gate, w_proj,
      ln_gain.astype(F32).reshape(1, dm), ln_bias.astype(F32).reshape(1, dm))


def kernel(x, p, w_in, w_out, ret_gn_gain, w_ple_gate, w_ple_proj, ln_gain, ln_bias):
    b, s, dm = x.shape
    h = x
    for i in range(DEPTH):
        h2d = h.reshape(b * s, dm)
        z = _input_projection(h2d, w_in[i].astype(F32), s)
        z3d = z.reshape(b, s, IN_WIDTH)
        attn = _moba_attention(z3d)
        ret = _retention(z3d, ret_gn_gain[i])
        out = _output_stage(h2d, attn.reshape(b * s, ATTN_WIDTH), ret.reshape(b * s, RET_WIDTH),
                            p[i].reshape(b * s, D_PLE), w_out[i].astype(F32), w_ple_gate[i].astype(F32),
                            w_ple_proj[i].astype(F32), ln_gain[i], ln_bias[i])
        h = out.reshape(b, s, dm)
    return h
```

```python
import functools

import jax
import jax.numpy as jnp
import numpy as np
from jax import lax
from jax.experimental import pallas as pl
from jax.experimental.pallas import tpu as pltpu

F32 = jnp.float32
BF16 = jnp.bfloat16

D_MODEL = 1024
D_PLE = 256
ATTN_HEADS = 8
ATTN_HEAD_DIM = 64
ATTN_WIDTH = ATTN_HEADS * ATTN_HEAD_DIM
RET_HEADS = 4
RET_HEAD_DIM = 128
RET_WIDTH = RET_HEADS * RET_HEAD_DIM
IN_WIDTH = 4 * ATTN_WIDTH + 4 * RET_WIDTH
MOBA_BLOCK = 256
MOBA_TOPK = 3
ROPE_BASE = 10000.0
DEPTH = 1
DEEPNORM_ALPHA = (2.0 * DEPTH) ** 0.25
LN_EPS = 1e-5
GN_EPS = 1e-6

LANES = 128
HEADS_PER_STEP = LANES // ATTN_HEAD_DIM
MASKED = -1e30
LOG2_E = 1.4426950408889634
ONES_ROWS = 16
CHUNK_BLOCKS = 4
RET_CHUNK = 256
VMEM_LIMIT = 48 * 1024 * 1024


def _proj_kernel(x_ref, w32_ref, cos_ref, sin_ref, o_ref, w_ref, *, n_chunk):
    @pl.when(pl.program_id(0) == 0)
    def _cast_weights_once():
        for lo in range(0, w_ref.shape[1], n_chunk):
            w_ref[:, lo:lo + n_chunk] = w32_ref[:, lo:lo + n_chunk].astype(BF16)

    xb = x_ref[...].astype(BF16)
    d = RET_HEAD_DIM
    rot_lo = 4 * ATTN_WIDTH
    rot_mid = rot_lo + RET_WIDTH
    rot_hi = rot_mid + RET_WIDTH
    assert rot_lo % n_chunk == 0 and rot_hi % n_chunk == 0 and n_chunk % d == 0
    for lo in range(0, o_ref.shape[1], n_chunk):
        acc = jnp.dot(xb, w_ref[:, lo:lo + n_chunk], preferred_element_type=F32)
        if rot_lo <= lo < rot_hi:
            cos = cos_ref[...]
            sin = sin_ref[...]
            for col in range(lo, lo + n_chunk, d):
                head = acc[:, col - lo:col - lo + d]
                rotated = head * cos + pltpu.roll(head, d // 2, 1) * sin
                if col >= rot_mid:
                    rotated = rotated * (d ** -0.5)
                o_ref[:, col:col + d] = rotated.astype(o_ref.dtype)
        else:
            o_ref[:, lo:lo + n_chunk] = acc.astype(o_ref.dtype)


def _rotary_tables(s):
    half = RET_HEAD_DIM // 2
    freqs = ROPE_BASE ** (-np.arange(half, dtype=np.float64) / half)
    ang = np.arange(s, dtype=np.float64)[:, None] * freqs[None, :]
    cos, sin = np.cos(ang), np.sin(ang)
    return (np.concatenate([cos, cos], axis=-1).astype(np.float32),
            np.concatenate([-sin, sin], axis=-1).astype(np.float32))


def _resident(shape):
    return pl.BlockSpec(shape, lambda *_: (0,) * len(shape), pipeline_mode=pl.Buffered(1))


def _input_projection(x2d, w_in, seq, *, tm=512, n_chunk=1024):
    m, k = x2d.shape
    n = w_in.shape[1]
    cos, sin = _rotary_tables(seq)
    pos = pl.BlockSpec((tm, RET_HEAD_DIM), lambda i: (i % (seq // tm), 0))
    return pl.pallas_call(
        functools.partial(_proj_kernel, n_chunk=n_chunk),
        grid=(m // tm,),
        in_specs=[pl.BlockSpec((tm, k), lambda i: (i, 0)), _resident((k, n)), pos, pos],
        out_specs=pl.BlockSpec((tm, n), lambda i: (i, 0)),
        out_shape=jax.ShapeDtypeStruct((m, n), BF16),
        scratch_shapes=[pltpu.VMEM((k, n), BF16)],
        compiler_params=pltpu.CompilerParams(dimension_semantics=("arbitrary",),
                                             vmem_limit_bytes=VMEM_LIMIT),
        name="input_projection",
    )(x2d, w_in, cos, sin)


def _moba_kernel(q_ref, k_ref, v_ref, g_ref, onehot_ref, o_ref,
                 vtc_ref, qtb_ref, s0_ref, s1_ref, mrun_ref, mnow_ref, alpha_ref, acc_ref, *, nb):
    blk = MOBA_BLOCK
    hd = ATTN_HEAD_DIM
    nh = HEADS_PER_STEP
    width = nh * blk
    vrows = LANES + ONES_ROWS
    seq = nb * blk
    chunk = CHUNK_BLOCKS * blk
    n_chunks = nb // CHUNK_BLOCKS
    s_refs = (s0_ref, s1_ref)

    @pl.when((pl.program_id(0) == 0) & (pl.program_id(1) == 0))
    def _fill_constant_regions():
        vtc_ref[:, LANES:vrows, :] = jnp.ones((n_chunks, ONES_ROWS, chunk), BF16)
        qtb_ref[...] = jnp.zeros(qtb_ref.shape, BF16)

    def _prepare_head_pair():
        for n in range(nb):
            c, j = divmod(n, CHUNK_BLOCKS)
            vtc_ref[c, 0:LANES, j * blk:(j + 1) * blk] = (
                v_ref[0, n * blk:(n + 1) * blk, :].astype(F32).T.astype(BF16))
        key_blk = lax.broadcasted_iota(jnp.int32, (nb, seq), 1) // blk
        row_blk = lax.broadcasted_iota(jnp.int32, (nb, seq), 0)
        averager = jnp.where(key_blk == row_blk, 1.0 / blk, 0.0).astype(BF16)
        kmean = jnp.dot(averager, k_ref[0], preferred_element_type=F32)
        hi = kmean.astype(BF16)
        lo = (kmean - hi.astype(F32)).astype(BF16)
        lane = lax.broadcasted_iota(jnp.int32, (nb, LANES), 1)
        parts = []
        for h in range(nh):
            in_head = (lane >= h * hd) & (lane < (h + 1) * hd)
            parts += [jnp.where(in_head, hi, jnp.zeros_like(hi)), jnp.where(in_head, lo, jnp.zeros_like(lo))]
        kstack = jnp.concatenate(parts, axis=0)
        qts = []
        for n in range(nb):
            qt = (q_ref[0, n * blk:(n + 1) * blk, :].astype(F32) * (hd ** -0.5 * LOG2_E)).T.astype(BF16)
            for h in range(nh):
                qtb_ref[n, h * hd:(h + 1) * hd, h * blk:(h + 1) * blk] = qt[h * hd:(h + 1) * hd]
            qts.append(qt)
        gates = jnp.dot(kstack, jnp.concatenate(qts, axis=1), preferred_element_type=F32)
        past = row_blk < key_blk
        row_f = row_blk.astype(F32)
        own_f = jnp.where(row_blk == key_blk, 1.0, 0.0)
        for h in range(nh):
            g = gates[(2 * h) * nb:(2 * h + 1) * nb] + gates[(2 * h + 1) * nb:(2 * h + 2) * nb]
            g = jnp.where(past, g, -jnp.inf)
            picked = jnp.zeros((nb, seq), F32)
            for _ in range(MOBA_TOPK):
                top = jnp.max(g, axis=0, keepdims=True)
                first = jnp.min(jnp.where(g == top, row_f, float(nb)), axis=0, keepdims=True)
                pick = row_f == first
                picked = jnp.where(pick, 1.0, picked)
                g = jnp.where(pick, -jnp.inf, g)
            bias = jnp.where(jnp.where(past, picked, own_f) > 0.5, 0.0, MASKED).astype(BF16)
            for n in range(nb):
                qtb_ref[n, LANES:LANES + nb, h * blk:(h + 1) * blk] = bias[:, n * blk:(n + 1) * blk]

    _prepare_head_pair()

    key_pos = lax.broadcasted_iota(jnp.int32, (blk, width), 0)
    q_pos = lax.broadcasted_iota(jnp.int32, (blk, width), 1) & (blk - 1)
    causal = key_pos <= q_pos

    def rows_at(start, size):
        if isinstance(start, int):
            return pl.ds(start, size)
        return pl.ds(pl.multiple_of(start, blk), size)

    def scores(c, qi, slot, cnt, own_j=None):
        key_rows = rows_at(c * chunk, cnt * blk)
        keys_ext = jnp.concatenate([k_ref[0, key_rows, :], onehot_ref[key_rows, :]], axis=1)
        s = jnp.dot(keys_ext, qtb_ref[qi], preferred_element_type=F32)
        m8 = None
        for j in range(cnt):
            sj = s[j * blk:(j + 1) * blk]
            if j == own_j:
                sj = jnp.where(causal, sj, MASKED)
            s_refs[slot][j * blk:(j + 1) * blk, :] = sj
            mj = jnp.max(sj.reshape(blk // 8, 8, width), axis=0)
            m8 = mj if m8 is None else jnp.maximum(m8, mj)
        m_new = jnp.broadcast_to(jnp.max(m8, axis=0, keepdims=True), (8, width))
        if c > 0:
            m_old = mrun_ref[qi]
            m_new = jnp.maximum(m_old, m_new)
            alpha_ref[slot] = jnp.exp2(m_old - m_new)
        mrun_ref[qi] = m_new
        mnow_ref[slot] = m_new

    def values(c, qi, slot, cnt):
        p = jnp.exp2(s_refs[slot][0:cnt * blk, :] - mnow_ref[slot, 0:1, :]).astype(BF16)
        pv = jnp.dot(vtc_ref[c, :, 0:cnt * blk], p, preferred_element_type=F32)
        if c > 0:
            pv = acc_ref[qi] * alpha_ref[slot, 0:1, :] + pv
        acc_ref[qi] = pv

    def finish(qi):
        outs = []
        for h in range(nh):
            cols = slice(h * blk, (h + 1) * blk)
            outs.append(acc_ref[qi, h * hd:(h + 1) * hd, cols] / acc_ref[qi, LANES:LANES + 1, cols])
        attn = jnp.concatenate(outs, axis=0).T
        gate = g_ref[0, rows_at(qi * blk, blk), :].astype(F32)
        o_ref[0, rows_at(qi * blk, blk), :] = (attn * (gate * jax.nn.sigmoid(gate))).astype(o_ref.dtype)

    assert CHUNK_BLOCKS % 2 == 0 and nb % CHUNK_BLOCKS == 0

    def chunk_section(c):
        first = c * CHUNK_BLOCKS
        later = first + CHUNK_BLOCKS
        scores(c, first, 0, 1, own_j=0)
        for j in range(CHUNK_BLOCKS):
            if j + 1 < CHUNK_BLOCKS:
                scores(c, first + j + 1, (j + 1) % 2, j + 2, own_j=j + 1)
            elif later < nb:
                scores(c, later, 0, CHUNK_BLOCKS)
            values(c, first + j, j % 2, j + 1)
            finish(first + j)
        for qi in range(later, nb):
            if qi + 1 < nb:
                scores(c, qi + 1, (qi + 1 - later) % 2, CHUNK_BLOCKS)
            values(c, qi, (qi - later) % 2, CHUNK_BLOCKS)

    chunk_section(0)

    @pl.when(pl.program_id(1) >= 0)
    def _remaining_chunks():
        for c in range(1, n_chunks):
            chunk_section(c)


def _moba_attention(z3d):
    b, s, _ = z3d.shape
    nb = s // MOBA_BLOCK
    steps = ATTN_WIDTH // LANES
    width = HEADS_PER_STEP * MOBA_BLOCK
    vrows = LANES + ONES_ROWS
    kernel = functools.partial(_moba_kernel, nb=nb)
    full = lambda off: pl.BlockSpec((1, s, LANES), lambda bi, hp: (bi, 0, off * steps + hp))
    onehot = jnp.asarray(np.arange(s)[:, None] // MOBA_BLOCK == np.arange(LANES)[None, :], dtype=BF16)
    return pl.pallas_call(
        kernel,
        grid=(b, steps),
        in_specs=[full(0), full(1), full(2), full(3), _resident((s, LANES))],
        out_specs=pl.BlockSpec((1, s, LANES), lambda bi, hp: (bi, 0, hp)),
        out_shape=jax.ShapeDtypeStruct((b, s, ATTN_WIDTH), BF16),
        scratch_shapes=[pltpu.VMEM((nb // CHUNK_BLOCKS, vrows, CHUNK_BLOCKS * MOBA_BLOCK), BF16),
                        pltpu.VMEM((nb, 2 * LANES, width), BF16),
                        pltpu.VMEM((CHUNK_BLOCKS * MOBA_BLOCK, width), F32),
                        pltpu.VMEM((CHUNK_BLOCKS * MOBA_BLOCK, width), F32),
                        pltpu.VMEM((nb, 8, width), F32),
                        pltpu.VMEM((2, 8, width), F32),
                        pltpu.VMEM((2, 8, width), F32),
                        pltpu.VMEM((nb, vrows, width), F32)],
        compiler_params=pltpu.CompilerParams(dimension_semantics=("arbitrary",) * 2,
                                             vmem_limit_bytes=VMEM_LIMIT),
        name="moba_attention",
    )(z3d, z3d, z3d, z3d, onehot)


def _retention_kernel(q_ref, k_ref, v_ref, g_ref, decay_ref, xi_ref, zeta_ref, gc_ref, gain_ref,
                      o_ref, state_ref):
    c = pl.program_id(0)
    d = RET_HEAD_DIM

    @pl.when(c == 0)
    def _reset_state():
        state_ref[...] = jnp.zeros_like(state_ref)

    for bi in range(q_ref.shape[0]):
        for h in range(RET_HEADS):
            cols = slice(h * d, (h + 1) * d)
            qb = q_ref[bi, :, cols]
            kb = k_ref[bi, :, cols]
            vb = v_ref[bi, :, cols]
            scores = lax.dot_general(qb, kb, (((1,), (1,)), ((), ())), preferred_element_type=F32)
            inner = (scores * decay_ref[h]).astype(BF16)
            state = state_ref[bi, h]
            out = (jnp.dot(inner, vb, preferred_element_type=F32)
                   + jnp.dot(qb, state.astype(BF16), preferred_element_type=F32) * xi_ref[h])
            vz = (vb.astype(F32) * zeta_ref[h]).astype(BF16)
            state_ref[bi, h] = state * gc_ref[h] + lax.dot_general(
                kb, vz, (((0,), (0,)), ((), ())), preferred_element_type=F32)

            mu = jnp.mean(out, axis=-1, keepdims=True)
            cen = out - mu
            var = jnp.mean(cen * cen, axis=-1, keepdims=True)
            normed = cen * lax.rsqrt(var + GN_EPS) * gain_ref[h]
            gate = g_ref[bi, :, cols].astype(F32)
            o_ref[bi, :, cols] = (normed * (gate * jax.nn.sigmoid(gate))).astype(o_ref.dtype)


def _retention_constants():
    d, c = RET_HEAD_DIM, RET_CHUNK
    log_g = np.log1p(-np.exp2(-5.0 - np.arange(RET_HEADS, dtype=np.float64)))
    idx = np.arange(c, dtype=np.float64)
    rel = idx[:, None] - idx[None, :]
    decay = np.where(rel >= 0, np.exp(np.maximum(rel, 0.0)[None] * log_g[:, None, None]), 0.0)
    xi = np.exp((idx + 1.0)[None] * log_g[:, None])
    zeta = np.exp((c - 1.0 - idx)[None] * log_g[:, None])
    g_chunk = np.exp(c * log_g)
    bcast = lambda t: np.ascontiguousarray(np.broadcast_to(t[..., None], t.shape + (d,)), dtype=np.float32)
    return decay.astype(np.float32), bcast(xi), bcast(zeta), bcast(g_chunk[:, None])


def _retention(z3d, ret_gn_gain):
    b, s, _ = z3d.shape
    c, d = RET_CHUNK, RET_HEAD_DIM
    base = 4 * ATTN_WIDTH // RET_WIDTH
    decay, xi, zeta, gc = _retention_constants()
    gain = ret_gn_gain.astype(F32).reshape(RET_HEADS, 1, d)
    zspec = lambda off: pl.BlockSpec((b, c, RET_WIDTH), lambda ci: (0, ci, base + off))
    whole = lambda *shape: pl.BlockSpec(shape, lambda ci: (0,) * len(shape))
    return pl.pallas_call(
        _retention_kernel,
        grid=(s // c,),
        in_specs=[zspec(0), zspec(1), zspec(2), zspec(3),
                  whole(RET_HEADS, c, c), whole(RET_HEADS, c, d), whole(RET_HEADS, c, d),
                  whole(RET_HEADS, 1, d), whole(RET_HEADS, 1, d)],
        out_specs=pl.BlockSpec((b, c, RET_WIDTH), lambda ci: (0, ci, 0)),
        out_shape=jax.ShapeDtypeStruct((b, s, RET_WIDTH), BF16),
        scratch_shapes=[pltpu.VMEM((b, RET_HEADS, d, d), F32)],
        compiler_params=pltpu.CompilerParams(dimension_semantics=("arbitrary",),
                                             vmem_limit_bytes=VMEM_LIMIT),
        name="retention",
    )(z3d, z3d, z3d, z3d, decay, xi, zeta, gc, gain)


def _output_kernel(x_ref, a_ref, r_ref, p_ref, wo32_ref, wg32_ref, wp32_ref, gain_ref, bias_ref, o_ref,
                   wo_ref, wg_ref, wp_ref, *, sub):
    @pl.when(pl.program_id(0) == 0)
    def _cast_weights_once():
        wo_ref[...] = wo32_ref[...].astype(BF16)
        wg_ref[...] = wg32_ref[...].astype(BF16)
        wp_ref[...] = wp32_ref[...].astype(BF16)

    def matmuls(rows):
        mix = (jnp.dot(a_ref[rows, :], wo_ref[0:ATTN_WIDTH, :], preferred_element_type=F32)
               + jnp.dot(r_ref[rows, :], wo_ref[ATTN_WIDTH:, :], preferred_element_type=F32))
        u = DEEPNORM_ALPHA * x_ref[rows, :] + mix
        ple = jnp.dot(p_ref[rows, :].astype(BF16), wp_ref[...], preferred_element_type=F32)
        gate_in = jnp.dot(u.astype(BF16), wg_ref[...], preferred_element_type=F32)
        return u, ple, gate_in

    def tail(rows, u, ple, gate_in):
        u = u + jax.nn.sigmoid(gate_in) * ple
        mu = jnp.mean(u, axis=-1, keepdims=True)
        cen = u - mu
        var = jnp.mean(cen * cen, axis=-1, keepdims=True)
        o_ref[rows, :] = cen * lax.rsqrt(var + LN_EPS) * gain_ref[...] + bias_ref[...]

    n_sub = o_ref.shape[0] // sub
    rows = [slice(r * sub, (r + 1) * sub) for r in range(n_sub)]
    pending = matmuls(rows[0])
    for r in range(n_sub):
        following = matmuls(rows[r + 1]) if r + 1 < n_sub else None
        tail(rows[r], *pending)
        pending = following


def _output_stage(x2d, attn2d, ret2d, p2d, w_out, w_gate, w_proj, ln_gain, ln_bias, *, tm=1024, sub=256):
    m, dm = x2d.shape
    rows = lambda width: pl.BlockSpec((tm, width), lambda i: (i, 0))
    return pl.pallas_call(
        functools.partial(_output_kernel, sub=sub),
        grid=(m // tm,),
        in_specs=[rows(dm), rows(ATTN_WIDTH), rows(RET_WIDTH), rows(D_PLE),
                  _resident(w_out.shape), _resident(w_gate.shape), _resident(w_proj.shape),
                  _resident((1, dm)), _resident((1, dm))],
        out_specs=rows(dm),
        out_shape=jax.ShapeDtypeStruct((m, dm), F32),
        scratch_shapes=[pltpu.VMEM(w_out.shape, BF16), pltpu.VMEM(w_gate.shape, BF16),
                        pltpu.VMEM(w_proj.shape, BF16)],
        compiler_params=pltpu.CompilerParams(dimension_semantics=("arbitrary",),
                                             vmem_limit_bytes=VMEM_LIMIT),
        name="output_stage",
    )(x2d, attn2d, ret2d, p2d, w_out, w_gate, w_proj,
      ln_gain.astype(F32).reshape(1, dm), ln_bias.astype(F32).reshape(1, dm))


def kernel(x, p, w_in, w_out, ret_gn_gain, w_ple_gate, w_ple_proj, ln_gain, ln_bias):
    b, s, dm = x.shape
    h = x
    for i in range(DEPTH):
        h2d = h.reshape(b * s, dm)
        z = _input_projection(h2d, w_in[i].astype(F32), s)
        z3d = z.reshape(b, s, IN_WIDTH)
        attn = _moba_attention(z3d)
        ret = _retention(z3d, ret_gn_gain[i])
        out = _output_stage(h2d, attn.reshape(b * s, ATTN_WIDTH), ret.reshape(b * s, RET_WIDTH),
                            p[i].reshape(b * s, D_PLE), w_out[i].astype(F32), w_ple_gate[i].astype(F32),
                            w_ple_proj[i].astype(F32), ln_gain[i], ln_bias[i])
        h = out.reshape(b, s, dm)
    return h
```

```python
import functools

import jax
import jax.numpy as jnp
import numpy as np
from jax import lax
from jax.experimental import pallas as pl
from jax.experimental.pallas import tpu as pltpu

F32 = jnp.float32
BF16 = jnp.bfloat16

D_MODEL = 1024
D_PLE = 256
ATTN_HEADS = 8
ATTN_HEAD_DIM = 64
ATTN_WIDTH = ATTN_HEADS * ATTN_HEAD_DIM
RET_HEADS = 4
RET_HEAD_DIM = 128
RET_WIDTH = RET_HEADS * RET_HEAD_DIM
IN_WIDTH = 4 * ATTN_WIDTH + 4 * RET_WIDTH
MOBA_BLOCK = 256
MOBA_TOPK = 3
ROPE_BASE = 10000.0
DEPTH = 1
DEEPNORM_ALPHA = (2.0 * DEPTH) ** 0.25
LN_EPS = 1e-5
GN_EPS = 1e-6

LANES = 128
HEADS_PER_STEP = LANES // ATTN_HEAD_DIM
MASKED = -1e30
LOG2_E = 1.4426950408889634
ONES_ROWS = 16
CHUNK_BLOCKS = 4
RET_CHUNK = 256
VMEM_LIMIT = 48 * 1024 * 1024


def _proj_kernel(x_ref, w32_ref, cos_ref, sin_ref, o_ref, w_ref, *, n_chunk):
    @pl.when(pl.program_id(0) == 0)
    def _cast_weights_once():
        for lo in range(0, w_ref.shape[1], n_chunk):
            w_ref[:, lo:lo + n_chunk] = w32_ref[:, lo:lo + n_chunk].astype(BF16)

    xb = x_ref[...].astype(BF16)
    d = RET_HEAD_DIM
    rot_lo = 4 * ATTN_WIDTH
    rot_mid = rot_lo + RET_WIDTH
    rot_hi = rot_mid + RET_WIDTH
    assert rot_lo % n_chunk == 0 and rot_hi % n_chunk == 0 and n_chunk % d == 0
    for lo in range(0, o_ref.shape[1], n_chunk):
        acc = jnp.dot(xb, w_ref[:, lo:lo + n_chunk], preferred_element_type=F32)
        if rot_lo <= lo < rot_hi:
            cos = cos_ref[...]
            sin = sin_ref[...]
            for col in range(lo, lo + n_chunk, d):
                head = acc[:, col - lo:col - lo + d]
                rotated = head * cos + pltpu.roll(head, d // 2, 1) * sin
                if col >= rot_mid:
                    rotated = rotated * (d ** -0.5)
                o_ref[:, col:col + d] = rotated.astype(o_ref.dtype)
        else:
            o_ref[:, lo:lo + n_chunk] = acc.astype(o_ref.dtype)


def _rotary_tables(s):
    half = RET_HEAD_DIM // 2
    freqs = ROPE_BASE ** (-np.arange(half, dtype=np.float64) / half)
    ang = np.arange(s, dtype=np.float64)[:, None] * freqs[None, :]
    cos, sin = np.cos(ang), np.sin(ang)
    return (np.concatenate([cos, cos], axis=-1).astype(np.float32),
            np.concatenate([-sin, sin], axis=-1).astype(np.float32))


def _resident(shape):
    return pl.BlockSpec(shape, lambda *_: (0,) * len(shape), pipeline_mode=pl.Buffered(1))


def _input_projection(x2d, w_in, seq, *, tm=512, n_chunk=1024):
    m, k = x2d.shape
    n = w_in.shape[1]
    cos, sin = _rotary_tables(seq)
    pos = pl.BlockSpec((tm, RET_HEAD_DIM), lambda i: (i % (seq // tm), 0))
    return pl.pallas_call(
        functools.partial(_proj_kernel, n_chunk=n_chunk),
        grid=(m // tm,),
        in_specs=[pl.BlockSpec((tm, k), lambda i: (i, 0)), _resident((k, n)), pos, pos],
        out_specs=pl.BlockSpec((tm, n), lambda i: (i, 0)),
        out_shape=jax.ShapeDtypeStruct((m, n), BF16),
        scratch_shapes=[pltpu.VMEM((k, n), BF16)],
        compiler_params=pltpu.CompilerParams(dimension_semantics=("arbitrary",),
                                             vmem_limit_bytes=VMEM_LIMIT),
        name="input_projection",
    )(x2d, w_in, cos, sin)


def _moba_kernel(q_ref, k_ref, v_ref, g_ref, onehot_ref, o_ref,
                 vtc_ref, qtb_ref, s0_ref, s1_ref, mrun_ref, mnow_ref, alpha_ref, acc_ref, *, nb):
    blk = MOBA_BLOCK
    hd = ATTN_HEAD_DIM
    nh = HEADS_PER_STEP
    width = nh * blk
    vrows = LANES + ONES_ROWS
    seq = nb * blk
    chunk = CHUNK_BLOCKS * blk
    n_chunks = nb // CHUNK_BLOCKS
    s_refs = (s0_ref, s1_ref)

    @pl.when((pl.program_id(0) == 0) & (pl.program_id(1) == 0))
    def _fill_constant_regions():
        vtc_ref[:, LANES:vrows, :] = jnp.ones((n_chunks, ONES_ROWS, chunk), BF16)
        qtb_ref[...] = jnp.zeros(qtb_ref.shape, BF16)

    def _prepare_head_pair():
        for n in range(nb):
            c, j = divmod(n, CHUNK_BLOCKS)
            vtc_ref[c, 0:LANES, j * blk:(j + 1) * blk] = (
                v_ref[0, n * blk:(n + 1) * blk, :].astype(F32).T.astype(BF16))
        key_blk = lax.broadcasted_iota(jnp.int32, (nb, seq), 1) // blk
        row_blk = lax.broadcasted_iota(jnp.int32, (nb, seq), 0)
        averager = jnp.where(key_blk == row_blk, 1.0 / blk, 0.0).astype(BF16)
        kmean = jnp.dot(averager, k_ref[0], preferred_element_type=F32)
        hi = kmean.astype(BF16)
        lo = (kmean - hi.astype(F32)).astype(BF16)
        lane = lax.broadcasted_iota(jnp.int32, (nb, LANES), 1)
        parts = []
        for h in range(nh):
            in_head = (lane >= h * hd) & (lane < (h + 1) * hd)
            parts += [jnp.where(in_head, hi, jnp.zeros_like(hi)), jnp.where(in_head, lo, jnp.zeros_like(lo))]
        kstack = jnp.concatenate(parts, axis=0)
        qts = []
        for n in range(nb):
            qt = (q_ref[0, n * blk:(n + 1) * blk, :].astype(F32) * (hd ** -0.5 * LOG2_E)).T.astype(BF16)
            for h in range(nh):
                qtb_ref[n, h * hd:(h + 1) * hd, h * blk:(h + 1) * blk] = qt[h * hd:(h + 1) * hd]
            qts.append(qt)
        gates = jnp.dot(kstack, jnp.concatenate(qts, axis=1), preferred_element_type=F32)
        past = row_blk < key_blk
        row_f = row_blk.astype(F32)
        own_f = jnp.where(row_blk == key_blk, 1.0, 0.0)
        for h in range(nh):
            g = gates[(2 * h) * nb:(2 * h + 1) * nb] + gates[(2 * h + 1) * nb:(2 * h + 2) * nb]
            g = jnp.where(past, g, -jnp.inf)
            picked = jnp.zeros((nb, seq), F32)
            for _ in range(MOBA_TOPK):
                top = jnp.max(g, axis=0, keepdims=True)
                first = jnp.min(jnp.where(g == top, row_f, float(nb)), axis=0, keepdims=True)
                pick = row_f == first
                picked = jnp.where(pick, 1.0, picked)
                g = jnp.where(pick, -jnp.inf, g)
            bias = jnp.where(jnp.where(past, picked, own_f) > 0.5, 0.0, MASKED).astype(BF16)
            for n in range(nb):
                qtb_ref[n, LANES:LANES + nb, h * blk:(h + 1) * blk] = bias[:, n * blk:(n + 1) * blk]

    _prepare_head_pair()

    key_pos = lax.broadcasted_iota(jnp.int32, (blk, width), 0)
    q_pos = lax.broadcasted_iota(jnp.int32, (blk, width), 1) & (blk - 1)
    causal = key_pos <= q_pos

    def rows_at(start, size):
        if isinstance(start, int):
            return pl.ds(start, size)
        return pl.ds(pl.multiple_of(start, blk), size)

    def scores(c, qi, slot, cnt, own_j=None):
        key_rows = rows_at(c * chunk, cnt * blk)
        keys_ext = jnp.concatenate([k_ref[0, key_rows, :], onehot_ref[key_rows, :]], axis=1)
        s = jnp.dot(keys_ext, qtb_ref[qi], preferred_element_type=F32)
        m8 = None
        for j in range(cnt):
            sj = s[j * blk:(j + 1) * blk]
            if j == own_j:
                sj = jnp.where(causal, sj, MASKED)
            s_refs[slot][j * blk:(j + 1) * blk, :] = sj
            mj = jnp.max(sj.reshape(blk // 8, 8, width), axis=0)
            m8 = mj if m8 is None else jnp.maximum(m8, mj)
        m_new = jnp.broadcast_to(jnp.max(m8, axis=0, keepdims=True), (8, width))
        if c > 0:
            m_old = mrun_ref[qi]
            m_new = jnp.maximum(m_old, m_new)
            alpha_ref[slot] = jnp.exp2(m_old - m_new)
        mrun_ref[qi] = m_new
        mnow_ref[slot] = m_new

    def values(c, qi, slot, cnt):
        p = jnp.exp2(s_refs[slot][0:cnt * blk, :] - mnow_ref[slot, 0:1, :]).astype(BF16)
        pv = jnp.dot(vtc_ref[c, :, 0:cnt * blk], p, preferred_element_type=F32)
        if c > 0:
            pv = acc_ref[qi] * alpha_ref[slot, 0:1, :] + pv
        acc_ref[qi] = pv

    def finish(qi):
        outs = []
        for h in range(nh):
            cols = slice(h * blk, (h + 1) * blk)
            outs.append(acc_ref[qi, h * hd:(h + 1) * hd, cols] / acc_ref[qi, LANES:LANES + 1, cols])
        attn = jnp.concatenate(outs, axis=0).T
        gate = g_ref[0, rows_at(qi * blk, blk), :].astype(F32)
        o_ref[0, rows_at(qi * blk, blk), :] = (attn * (gate * jax.nn.sigmoid(gate))).astype(o_ref.dtype)

    assert CHUNK_BLOCKS % 2 == 0 and nb % CHUNK_BLOCKS == 0

    def chunk_section(c):
        first = c * CHUNK_BLOCKS
        later = first + CHUNK_BLOCKS
        scores(c, first, 0, 1, own_j=0)
        for j in range(CHUNK_BLOCKS):
            if j + 1 < CHUNK_BLOCKS:
                scores(c, first + j + 1, (j + 1) % 2, j + 2, own_j=j + 1)
            elif later < nb:
                scores(c, later, 0, CHUNK_BLOCKS)
            values(c, first + j, j % 2, j + 1)
            finish(first + j)
        for qi in range(later, nb):
            if qi + 1 < nb:
                scores(c, qi + 1, (qi + 1 - later) % 2, CHUNK_BLOCKS)
            values(c, qi, (qi - later) % 2, CHUNK_BLOCKS)

    chunk_section(0)

    @pl.when(pl.program_id(1) >= 0)
    def _remaining_chunks():
        for c in range(1, n_chunks):
            chunk_section(c)


def _moba_attention(z3d):
    b, s, _ = z3d.shape
    nb = s // MOBA_BLOCK
    steps = ATTN_WIDTH // LANES
    width = HEADS_PER_STEP * MOBA_BLOCK
    vrows = LANES + ONES_ROWS
    kernel = functools.partial(_moba_kernel, nb=nb)
    full = lambda off: pl.BlockSpec((1, s, LANES), lambda bi, hp: (bi, 0, off * steps + hp))
    onehot = jnp.asarray(np.arange(s)[:, None] // MOBA_BLOCK == np.arange(LANES)[None, :], dtype=BF16)
    return pl.pallas_call(
        kernel,
        grid=(b, steps),
        in_specs=[full(0), full(1), full(2), full(3), _resident((s, LANES))],
        out_specs=pl.BlockSpec((1, s, LANES), lambda bi, hp: (bi, 0, hp)),
        out_shape=jax.ShapeDtypeStruct((b, s, ATTN_WIDTH), BF16),
        scratch_shapes=[pltpu.VMEM((nb // CHUNK_BLOCKS, vrows, CHUNK_BLOCKS * MOBA_BLOCK), BF16),
                        pltpu.VMEM((nb, 2 * LANES, width), BF16),
                        pltpu.VMEM((CHUNK_BLOCKS * MOBA_BLOCK, width), F32),
                        pltpu.VMEM((CHUNK_BLOCKS * MOBA_BLOCK, width), F32),
                        pltpu.VMEM((nb, 8, width), F32),
                        pltpu.VMEM((2, 8, width), F32),
                        pltpu.VMEM((2, 8, width), F32),
                        pltpu.VMEM((nb, vrows, width), F32)],
        compiler_params=pltpu.CompilerParams(dimension_semantics=("arbitrary",) * 2,
                                             vmem_limit_bytes=VMEM_LIMIT),
        name="moba_attention",
    )(z3d, z3d, z3d, z3d, onehot)


def _retention_kernel(q_ref, k_ref, v_ref, g_ref, decay_ref, xi_ref, zeta_ref, gc_ref, gain_ref,
                      o_ref, state_ref):
    c = pl.program_id(0)
    d = RET_HEAD_DIM

    @pl.when(c == 0)
    def _reset_state():
        state_ref[...] = jnp.zeros_like(state_ref)

    for bi in range(q_ref.shape[0]):
        for h in range(RET_HEADS):
            cols = slice(h * d, (h + 1) * d)
            qb = q_ref[bi, :, cols]
            kb = k_ref[bi, :, cols]
            vb = v_ref[bi, :, cols]
            scores = lax.dot_general(qb, kb, (((1,), (1,)), ((), ())), preferred_element_type=F32)
            inner = (scores * decay_ref[h]).astype(BF16)
            state = state_ref[bi, h]
            out = (jnp.dot(inner, vb, preferred_element_type=F32)
                   + jnp.dot(qb, state.astype(BF16), preferred_element_type=F32) * xi_ref[h])
            vz = (vb.astype(F32) * zeta_ref[h]).astype(BF16)
            state_ref[bi, h] = state * gc_ref[h] + lax.dot_general(
                kb, vz, (((0,), (0,)), ((), ())), preferred_element_type=F32)

            mu = jnp.mean(out, axis=-1, keepdims=True)
            cen = out - mu
            var = jnp.mean(cen * cen, axis=-1, keepdims=True)
            normed = cen * lax.rsqrt(var + GN_EPS) * gain_ref[h]
            gate = g_ref[bi, :, cols].astype(F32)
            o_ref[bi, :, cols] = (normed * (gate * jax.nn.sigmoid(gate))).astype(o_ref.dtype)


def _retention_constants():
    d, c = RET_HEAD_DIM, RET_CHUNK
    log_g = np.log1p(-np.exp2(-5.0 - np.arange(RET_HEADS, dtype=np.float64)))
    idx = np.arange(c, dtype=np.float64)
    rel = idx[:, None] - idx[None, :]
    decay = np.where(rel >= 0, np.exp(np.maximum(rel, 0.0)[None] * log_g[:, None, None]), 0.0)
    xi = np.exp((idx + 1.0)[None] * log_g[:, None])
    zeta = np.exp((c - 1.0 - idx)[None] * log_g[:, None])
    g_chunk = np.exp(c * log_g)
    bcast = lambda t: np.ascontiguousarray(np.broadcast_to(t[..., None], t.shape + (d,)), dtype=np.float32)
    return decay.astype(np.float32), bcast(xi), bcast(zeta), bcast(g_chunk[:, None])


def _retention(z3d, ret_gn_gain):
    b, s, _ = z3d.shape
    c, d = RET_CHUNK, RET_HEAD_DIM
    base = 4 * ATTN_WIDTH // RET_WIDTH
    decay, xi, zeta, gc = _retention_constants()
    gain = ret_gn_gain.astype(F32).reshape(RET_HEADS, 1, d)
    zspec = lambda off: pl.BlockSpec((b, c, RET_WIDTH), lambda ci: (0, ci, base + off))
    whole = lambda *shape: pl.BlockSpec(shape, lambda ci: (0,) * len(shape))
    return pl.pallas_call(
        _retention_kernel,
        grid=(s // c,),
        in_specs=[zspec(0), zspec(1), zspec(2), zspec(3),
                  whole(RET_HEADS, c, c), whole(RET_HEADS, c, d), whole(RET_HEADS, c, d),
                  whole(RET_HEADS, 1, d), whole(RET_HEADS, 1, d)],
        out_specs=pl.BlockSpec((b, c, RET_WIDTH), lambda ci: (0, ci, 0)),
        out_shape=jax.ShapeDtypeStruct((b, s, RET_WIDTH), BF16),
        scratch_shapes=[pltpu.VMEM((b, RET_HEADS, d, d), F32)],
        compiler_params=pltpu.CompilerParams(dimension_semantics=("arbitrary",),
                                             vmem_limit_bytes=VMEM_LIMIT),
        name="retention",
    )(z3d, z3d, z3d, z3d, decay, xi, zeta, gc, gain)


def _output_kernel(x_ref, a_ref, r_ref, p_ref, wo32_ref, wg32_ref, wp32_ref, gain_ref, bias_ref, o_hbm,
                   wo_ref, wg_ref, wp_ref, pend_u, pend_ple, pend_g, ostage, osem, *, sub):
    i = pl.program_id(0)
    n_steps = pl.num_programs(0)
    tm = x_ref.shape[0]
    n_sub = tm // sub
    slot = i % 2
    base = i * tm

    def out_copy(slot, first_pos, n_rows, row0):
        return pltpu.make_async_copy(ostage.at[slot, pl.ds(first_pos, n_rows), :],
                                     o_hbm.at[pl.ds(row0, n_rows), :], osem.at[slot])

    @pl.when(i == 0)
    def _first_step():
        wo_ref[...] = wo32_ref[...].astype(BF16)
        wg_ref[...] = wg32_ref[...].astype(BF16)
        wp_ref[...] = wp32_ref[...].astype(BF16)
        pend_u[...] = jnp.zeros(pend_u.shape, F32)
        pend_ple[...] = jnp.zeros(pend_ple.shape, F32)
        pend_g[...] = jnp.zeros(pend_g.shape, F32)

    @pl.when(i == 2)
    def _wait_first_copy():
        out_copy(slot, sub, tm - sub, 0).wait()

    @pl.when(i > 2)
    def _wait_copy():
        out_copy(slot, 0, tm, 0).wait()

    def matmuls(rows):
        mix = (jnp.dot(a_ref[rows, :], wo_ref[0:ATTN_WIDTH, :], preferred_element_type=F32)
               + jnp.dot(r_ref[rows, :], wo_ref[ATTN_WIDTH:, :], preferred_element_type=F32))
        u = DEEPNORM_ALPHA * x_ref[rows, :] + mix
        ple = jnp.dot(p_ref[rows, :].astype(BF16), wp_ref[...], preferred_element_type=F32)
        gate_in = jnp.dot(u.astype(BF16), wg_ref[...], preferred_element_type=F32)
        return u, ple, gate_in

    def tail(slot, pos, u, ple, gate_in):
        u = u + jax.nn.sigmoid(gate_in) * ple
        mu = jnp.mean(u, axis=-1, keepdims=True)
        cen = u - mu
        var = jnp.mean(cen * cen, axis=-1, keepdims=True)
        ostage[slot, pos * sub:(pos + 1) * sub, :] = cen * lax.rsqrt(var + LN_EPS) * gain_ref[...] + bias_ref[...]

    rows = [slice(r * sub, (r + 1) * sub) for r in range(n_sub)]
    following = matmuls(rows[0])
    tail(slot, 0, pend_u[...], pend_ple[...], pend_g[...])
    for r in range(n_sub - 1):
        current = following
        following = matmuls(rows[r + 1])
        tail(slot, r + 1, *current)
    pend_u[...], pend_ple[...], pend_g[...] = following

    @pl.when(i == 0)
    def _start_first_copy():
        out_copy(slot, sub, tm - sub, 0).start()

    @pl.when(i > 0)
    def _start_copy():
        out_copy(slot, 0, tm, pl.multiple_of(base - sub, sub)).start()

    @pl.when(i == n_steps - 1)
    def _last_step():
        out_copy(1 - slot, 0, tm, 0).wait()
        tail(1 - slot, 0, pend_u[...], pend_ple[...], pend_g[...])
        last = out_copy(1 - slot, 0, sub, pl.multiple_of(base + tm - sub, sub))
        last.start()
        out_copy(slot, 0, tm, 0).wait()
        last.wait()


def _output_stage(x2d, attn2d, ret2d, p2d, w_out, w_gate, w_proj, ln_gain, ln_bias, *, tm=1024, sub=256):
    m, dm = x2d.shape
    assert m // tm >= 3
    rows = lambda width: pl.BlockSpec((tm, width), lambda i: (i, 0))
    return pl.pallas_call(
        functools.partial(_output_kernel, sub=sub),
        grid=(m // tm,),
        in_specs=[rows(dm), rows(ATTN_WIDTH), rows(RET_WIDTH), rows(D_PLE),
                  _resident(w_out.shape), _resident(w_gate.shape), _resident(w_proj.shape),
                  _resident((1, dm)), _resident((1, dm))],
        out_specs=pl.BlockSpec(memory_space=pl.ANY),
        out_shape=jax.ShapeDtypeStruct((m, dm), F32),
        scratch_shapes=[pltpu.VMEM(w_out.shape, BF16), pltpu.VMEM(w_gate.shape, BF16),
                        pltpu.VMEM(w_proj.shape, BF16),
                        pltpu.VMEM((sub, dm), F32), pltpu.VMEM((sub, dm), F32), pltpu.VMEM((sub, dm), F32),
                        pltpu.VMEM((2, tm, dm), F32), pltpu.SemaphoreType.DMA((2,))],
        compiler_params=pltpu.CompilerParams(dimension_semantics=("arbitrary",),
                                             vmem_limit_bytes=VMEM_LIMIT),
        name="output_stage",
    )(x2d, attn2d, ret2d, p2d, w_out, w_gate, w_proj,
      ln_gain.astype(F32).reshape(1, dm), ln_bias.astype(F32).reshape(1, dm))


def kernel(x, p, w_in, w_out, ret_gn_gain, w_ple_gate, w_ple_proj, ln_gain, ln_bias):
    b, s, dm = x.shape
    h = x
    for i in range(DEPTH):
        h2d = h.reshape(b * s, dm)
        z = _input_projection(h2d, w_in[i].astype(F32), s)
        z3d = z.reshape(b, s, IN_WIDTH)
        attn = _moba_attention(z3d)
        ret = _retention(z3d, ret_gn_gain[i])
        out = _output_stage(h2d, attn.reshape(b * s, ATTN_WIDTH), ret.reshape(b * s, RET_WIDTH),
                            p[i].reshape(b * s, D_PLE), w_out[i].astype(F32), w_ple_gate[i].astype(F32),
                            w_ple_proj[i].astype(F32), ln_gain[i], ln_bias[i])
        h = out.reshape(b, s, dm)
    return h
```

```python
import functools

import jax
import jax.numpy as jnp
import numpy as np
from jax import lax
from jax.experimental import pallas as pl
from jax.experimental.pallas import tpu as pltpu

F32 = jnp.float32
BF16 = jnp.bfloat16

D_MODEL = 1024
D_PLE = 256
ATTN_HEADS = 8
ATTN_HEAD_DIM = 64
ATTN_WIDTH = ATTN_HEADS * ATTN_HEAD_DIM
RET_HEADS = 4
RET_HEAD_DIM = 128
RET_WIDTH = RET_HEADS * RET_HEAD_DIM
IN_WIDTH = 4 * ATTN_WIDTH + 4 * RET_WIDTH
MOBA_BLOCK = 256
MOBA_TOPK = 3
ROPE_BASE = 10000.0
DEPTH = 1
DEEPNORM_ALPHA = (2.0 * DEPTH) ** 0.25
LN_EPS = 1e-5
GN_EPS = 1e-6

LANES = 128
HEADS_PER_STEP = LANES // ATTN_HEAD_DIM
MASKED = -1e30
LOG2_E = 1.4426950408889634
ONES_ROWS = 16
CHUNK_BLOCKS = 4
RET_CHUNK = 256
VMEM_LIMIT = 48 * 1024 * 1024


def _proj_kernel(x_ref, w_hbm, cos_ref, sin_ref, o_ref, w_ref, w32_ref, w_sem, *, n_chunk):
    d = RET_HEAD_DIM
    rot_lo = 4 * ATTN_WIDTH
    rot_mid = rot_lo + RET_WIDTH
    rot_hi = rot_mid + RET_WIDTH
    n_chunks = o_ref.shape[1] // n_chunk
    assert rot_lo % n_chunk == 0 and rot_hi % n_chunk == 0 and n_chunk % d == 0

    def w_copy(c):
        return pltpu.make_async_copy(w_hbm.at[:, pl.ds(c * n_chunk, n_chunk)], w32_ref.at[c], w_sem.at[c])

    def tile(first_step):
        xb = x_ref[...].astype(BF16)
        if first_step:
            w_copy(0).start()
        for c in range(n_chunks):
            lo = c * n_chunk
            if first_step:
                w_copy(c).wait()
                if c + 1 < n_chunks:
                    w_copy(c + 1).start()
                w_ref[:, lo:lo + n_chunk] = w32_ref[c].astype(BF16)
            acc = jnp.dot(xb, w_ref[:, lo:lo + n_chunk], preferred_element_type=F32)
            if rot_lo <= lo < rot_hi:
                cos = cos_ref[...]
                sin = sin_ref[...]
                for col in range(lo, lo + n_chunk, d):
                    head = acc[:, col - lo:col - lo + d]
                    rotated = head * cos + pltpu.roll(head, d // 2, 1) * sin
                    if col >= rot_mid:
                        rotated = rotated * (d ** -0.5)
                    o_ref[:, col:col + d] = rotated.astype(o_ref.dtype)
            else:
                o_ref[:, lo:lo + n_chunk] = acc.astype(o_ref.dtype)

    @pl.when(pl.program_id(0) == 0)
    def _first_step():
        tile(True)

    @pl.when(pl.program_id(0) > 0)
    def _later_steps():
        tile(False)


def _rotary_tables(s):
    half = RET_HEAD_DIM // 2
    freqs = ROPE_BASE ** (-np.arange(half, dtype=np.float64) / half)
    ang = np.arange(s, dtype=np.float64)[:, None] * freqs[None, :]
    cos, sin = np.cos(ang), np.sin(ang)
    return (np.concatenate([cos, cos], axis=-1).astype(np.float32),
            np.concatenate([-sin, sin], axis=-1).astype(np.float32))


def _resident(shape):
    return pl.BlockSpec(shape, lambda *_: (0,) * len(shape), pipeline_mode=pl.Buffered(1))


def _input_projection(x2d, w_in, seq, *, tm=512, n_chunk=1024):
    m, k = x2d.shape
    n = w_in.shape[1]
    cos, sin = _rotary_tables(seq)
    pos = pl.BlockSpec((tm, RET_HEAD_DIM), lambda i: (i % (seq // tm), 0))
    return pl.pallas_call(
        functools.partial(_proj_kernel, n_chunk=n_chunk),
        grid=(m // tm,),
        in_specs=[pl.BlockSpec((tm, k), lambda i: (i, 0)), pl.BlockSpec(memory_space=pl.ANY), pos, pos],
        out_specs=pl.BlockSpec((tm, n), lambda i: (i, 0)),
        out_shape=jax.ShapeDtypeStruct((m, n), BF16),
        scratch_shapes=[pltpu.VMEM((k, n), BF16), pltpu.VMEM((n // n_chunk, k, n_chunk), F32),
                        pltpu.SemaphoreType.DMA((n // n_chunk,))],
        compiler_params=pltpu.CompilerParams(dimension_semantics=("arbitrary",),
                                             vmem_limit_bytes=VMEM_LIMIT),
        name="input_projection",
    )(x2d, w_in, cos, sin)


def _moba_kernel(q_ref, k_ref, v_ref, g_ref, onehot_ref, o_ref,
                 vtc_ref, qtb_ref, s0_ref, s1_ref, mrun_ref, mnow_ref, alpha_ref, acc_ref, *, nb):
    blk = MOBA_BLOCK
    hd = ATTN_HEAD_DIM
    nh = HEADS_PER_STEP
    width = nh * blk
    vrows = LANES + ONES_ROWS
    seq = nb * blk
    chunk = CHUNK_BLOCKS * blk
    n_chunks = nb // CHUNK_BLOCKS
    s_refs = (s0_ref, s1_ref)

    @pl.when((pl.program_id(0) == 0) & (pl.program_id(1) == 0))
    def _fill_constant_regions():
        vtc_ref[:, LANES:vrows, :] = jnp.ones((n_chunks, ONES_ROWS, chunk), BF16)
        qtb_ref[...] = jnp.zeros(qtb_ref.shape, BF16)

    def _prepare_head_pair():
        for n in range(nb):
            c, j = divmod(n, CHUNK_BLOCKS)
            vtc_ref[c, 0:LANES, j * blk:(j + 1) * blk] = (
                v_ref[0, n * blk:(n + 1) * blk, :].astype(F32).T.astype(BF16))
        key_blk = lax.broadcasted_iota(jnp.int32, (nb, seq), 1) // blk
        row_blk = lax.broadcasted_iota(jnp.int32, (nb, seq), 0)
        averager = jnp.where(key_blk == row_blk, 1.0 / blk, 0.0).astype(BF16)
        kmean = jnp.dot(averager, k_ref[0], preferred_element_type=F32)
        hi = kmean.astype(BF16)
        lo = (kmean - hi.astype(F32)).astype(BF16)
        lane = lax.broadcasted_iota(jnp.int32, (nb, LANES), 1)
        parts = []
        for h in range(nh):
            in_head = (lane >= h * hd) & (lane < (h + 1) * hd)
            parts += [jnp.where(in_head, hi, jnp.zeros_like(hi)), jnp.where(in_head, lo, jnp.zeros_like(lo))]
        kstack = jnp.concatenate(parts, axis=0)
        qts = []
        for n in range(nb):
            qt = (q_ref[0, n * blk:(n + 1) * blk, :].astype(F32) * (hd ** -0.5 * LOG2_E)).T.astype(BF16)
            for h in range(nh):
                qtb_ref[n, h * hd:(h + 1) * hd, h * blk:(h + 1) * blk] = qt[h * hd:(h + 1) * hd]
            qts.append(qt)
        gates = jnp.dot(kstack, jnp.concatenate(qts, axis=1), preferred_element_type=F32)
        past = row_blk < key_blk
        row_f = row_blk.astype(F32)
        own_f = jnp.where(row_blk == key_blk, 1.0, 0.0)
        for h in range(nh):
            g = gates[(2 * h) * nb:(2 * h + 1) * nb] + gates[(2 * h + 1) * nb:(2 * h + 2) * nb]
            g = jnp.where(past, g, -jnp.inf)
            picked = jnp.zeros((nb, seq), F32)
            for _ in range(MOBA_TOPK):
                top = jnp.max(g, axis=0, keepdims=True)
                first = jnp.min(jnp.where(g == top, row_f, float(nb)), axis=0, keepdims=True)
                pick = row_f == first
                picked = jnp.where(pick, 1.0, picked)
                g = jnp.where(pick, -jnp.inf, g)
            bias = jnp.where(jnp.where(past, picked, own_f) > 0.5, 0.0, MASKED).astype(BF16)
            for n in range(nb):
                qtb_ref[n, LANES:LANES + nb, h * blk:(h + 1) * blk] = bias[:, n * blk:(n + 1) * blk]

    _prepare_head_pair()

    key_pos = lax.broadcasted_iota(jnp.int32, (blk, width), 0)
    q_pos = lax.broadcasted_iota(jnp.int32, (blk, width), 1) & (blk - 1)
    causal = key_pos <= q_pos

    def rows_at(start, size):
        if isinstance(start, int):
            return pl.ds(start, size)
        return pl.ds(pl.multiple_of(start, blk), size)

    def scores(c, qi, slot, cnt, own_j=None):
        key_rows = rows_at(c * chunk, cnt * blk)
        keys_ext = jnp.concatenate([k_ref[0, key_rows, :], onehot_ref[key_rows, :]], axis=1)
        s = jnp.dot(keys_ext, qtb_ref[qi], preferred_element_type=F32)
        m8 = None
        for j in range(cnt):
            sj = s[j * blk:(j + 1) * blk]
            if j == own_j:
                sj = jnp.where(causal, sj, MASKED)
            s_refs[slot][j * blk:(j + 1) * blk, :] = sj
            mj = jnp.max(sj.reshape(blk // 8, 8, width), axis=0)
            m8 = mj if m8 is None else jnp.maximum(m8, mj)
        m_new = jnp.broadcast_to(jnp.max(m8, axis=0, keepdims=True), (8, width))
        if c > 0:
            m_old = mrun_ref[qi]
            m_new = jnp.maximum(m_old, m_new)
            alpha_ref[slot] = jnp.exp2(m_old - m_new)
        mrun_ref[qi] = m_new
        mnow_ref[slot] = m_new

    def values(c, qi, slot, cnt):
        p = jnp.exp2(s_refs[slot][0:cnt * blk, :] - mnow_ref[slot, 0:1, :]).astype(BF16)
        pv = jnp.dot(vtc_ref[c, :, 0:cnt * blk], p, preferred_element_type=F32)
        if c > 0:
            pv = acc_ref[qi] * alpha_ref[slot, 0:1, :] + pv
        acc_ref[qi] = pv

    def finish(qi):
        outs = []
        for h in range(nh):
            cols = slice(h * blk, (h + 1) * blk)
            outs.append(acc_ref[qi, h * hd:(h + 1) * hd, cols] / acc_ref[qi, LANES:LANES + 1, cols])
        attn = jnp.concatenate(outs, axis=0).T
        gate = g_ref[0, rows_at(qi * blk, blk), :].astype(F32)
        o_ref[0, rows_at(qi * blk, blk), :] = (attn * (gate * jax.nn.sigmoid(gate))).astype(o_ref.dtype)

    assert CHUNK_BLOCKS % 2 == 0 and nb % CHUNK_BLOCKS == 0

    def chunk_section(c):
        first = c * CHUNK_BLOCKS
        later = first + CHUNK_BLOCKS
        scores(c, first, 0, 1, own_j=0)
        for j in range(CHUNK_BLOCKS):
            if j + 1 < CHUNK_BLOCKS:
                scores(c, first + j + 1, (j + 1) % 2, j + 2, own_j=j + 1)
            elif later < nb:
                scores(c, later, 0, CHUNK_BLOCKS)
            values(c, first + j, j % 2, j + 1)
            finish(first + j)
        for qi in range(later, nb):
            if qi + 1 < nb:
                scores(c, qi + 1, (qi + 1 - later) % 2, CHUNK_BLOCKS)
            values(c, qi, (qi - later) % 2, CHUNK_BLOCKS)

    chunk_section(0)

    @pl.when(pl.program_id(1) >= 0)
    def _remaining_chunks():
        for c in range(1, n_chunks):
            chunk_section(c)


def _moba_attention(z3d):
    b, s, _ = z3d.shape
    nb = s // MOBA_BLOCK
    steps = ATTN_WIDTH // LANES
    width = HEADS_PER_STEP * MOBA_BLOCK
    vrows = LANES + ONES_ROWS
    kernel = functools.partial(_moba_kernel, nb=nb)
    full = lambda off: pl.BlockSpec((1, s, LANES), lambda bi, hp: (bi, 0, off * steps + hp))
    onehot = jnp.asarray(np.arange(s)[:, None] // MOBA_BLOCK == np.arange(LANES)[None, :], dtype=BF16)
    return pl.pallas_call(
        kernel,
        grid=(b, steps),
        in_specs=[full(0), full(1), full(2), full(3), _resident((s, LANES))],
        out_specs=pl.BlockSpec((1, s, LANES), lambda bi, hp: (bi, 0, hp)),
        out_shape=jax.ShapeDtypeStruct((b, s, ATTN_WIDTH), BF16),
        scratch_shapes=[pltpu.VMEM((nb // CHUNK_BLOCKS, vrows, CHUNK_BLOCKS * MOBA_BLOCK), BF16),
                        pltpu.VMEM((nb, 2 * LANES, width), BF16),
                        pltpu.VMEM((CHUNK_BLOCKS * MOBA_BLOCK, width), F32),
                        pltpu.VMEM((CHUNK_BLOCKS * MOBA_BLOCK, width), F32),
                        pltpu.VMEM((nb, 8, width), F32),
                        pltpu.VMEM((2, 8, width), F32),
                        pltpu.VMEM((2, 8, width), F32),
                        pltpu.VMEM((nb, vrows, width), F32)],
        compiler_params=pltpu.CompilerParams(dimension_semantics=("arbitrary",) * 2,
                                             vmem_limit_bytes=VMEM_LIMIT),
        name="moba_attention",
    )(z3d, z3d, z3d, z3d, onehot)


def _retention_kernel(q_ref, k_ref, v_ref, g_ref, decay_ref, xi_ref, zeta_ref, gc_ref, gain_ref,
                      o_ref, state_ref):
    c = pl.program_id(0)
    d = RET_HEAD_DIM

    @pl.when(c == 0)
    def _reset_state():
        state_ref[...] = jnp.zeros_like(state_ref)

    for bi in range(q_ref.shape[0]):
        for h in range(RET_HEADS):
            cols = slice(h * d, (h + 1) * d)
            qb = q_ref[bi, :, cols]
            kb = k_ref[bi, :, cols]
            vb = v_ref[bi, :, cols]
            scores = lax.dot_general(qb, kb, (((1,), (1,)), ((), ())), preferred_element_type=F32)
            inner = (scores * decay_ref[h]).astype(BF16)
            state = state_ref[bi, h]
            out = (jnp.dot(inner, vb, preferred_element_type=F32)
                   + jnp.dot(qb, state.astype(BF16), preferred_element_type=F32) * xi_ref[h])
            vz = (vb.astype(F32) * zeta_ref[h]).astype(BF16)
            state_ref[bi, h] = state * gc_ref[h] + lax.dot_general(
                kb, vz, (((0,), (0,)), ((), ())), preferred_element_type=F32)

            mu = jnp.mean(out, axis=-1, keepdims=True)
            cen = out - mu
            var = jnp.mean(cen * cen, axis=-1, keepdims=True)
            normed = cen * lax.rsqrt(var + GN_EPS) * gain_ref[h]
            gate = g_ref[bi, :, cols].astype(F32)
            o_ref[bi, :, cols] = (normed * (gate * jax.nn.sigmoid(gate))).astype(o_ref.dtype)


def _retention_constants():
    d, c = RET_HEAD_DIM, RET_CHUNK
    log_g = np.log1p(-np.exp2(-5.0 - np.arange(RET_HEADS, dtype=np.float64)))
    idx = np.arange(c, dtype=np.float64)
    rel = idx[:, None] - idx[None, :]
    decay = np.where(rel >= 0, np.exp(np.maximum(rel, 0.0)[None] * log_g[:, None, None]), 0.0)
    xi = np.exp((idx + 1.0)[None] * log_g[:, None])
    zeta = np.exp((c - 1.0 - idx)[None] * log_g[:, None])
    g_chunk = np.exp(c * log_g)
    bcast = lambda t: np.ascontiguousarray(np.broadcast_to(t[..., None], t.shape + (d,)), dtype=np.float32)
    return decay.astype(np.float32), bcast(xi), bcast(zeta), bcast(g_chunk[:, None])


def _retention(z3d, ret_gn_gain):
    b, s, _ = z3d.shape
    c, d = RET_CHUNK, RET_HEAD_DIM
    base = 4 * ATTN_WIDTH // RET_WIDTH
    decay, xi, zeta, gc = _retention_constants()
    gain = ret_gn_gain.astype(F32).reshape(RET_HEADS, 1, d)
    zspec = lambda off: pl.BlockSpec((b, c, RET_WIDTH), lambda ci: (0, ci, base + off))
    whole = lambda *shape: pl.BlockSpec(shape, lambda ci: (0,) * len(shape))
    return pl.pallas_call(
        _retention_kernel,
        grid=(s // c,),
        in_specs=[zspec(0), zspec(1), zspec(2), zspec(3),
                  whole(RET_HEADS, c, c), whole(RET_HEADS, c, d), whole(RET_HEADS, c, d),
                  whole(RET_HEADS, 1, d), whole(RET_HEADS, 1, d)],
        out_specs=pl.BlockSpec((b, c, RET_WIDTH), lambda ci: (0, ci, 0)),
        out_shape=jax.ShapeDtypeStruct((b, s, RET_WIDTH), BF16),
        scratch_shapes=[pltpu.VMEM((b, RET_HEADS, d, d), F32)],
        compiler_params=pltpu.CompilerParams(dimension_semantics=("arbitrary",),
                                             vmem_limit_bytes=VMEM_LIMIT),
        name="retention",
    )(z3d, z3d, z3d, z3d, decay, xi, zeta, gc, gain)


def _output_kernel(x_ref, a_ref, r_ref, p_ref, wo32_ref, wg32_ref, wp32_ref, gain_ref, bias_ref, o_ref,
                   wo_ref, wg_ref, wp_ref, *, sub):
    @pl.when(pl.program_id(0) == 0)
    def _cast_weights_once():
        wo_ref[...] = wo32_ref[...].astype(BF16)
        wg_ref[...] = wg32_ref[...].astype(BF16)
        wp_ref[...] = wp32_ref[...].astype(BF16)

    def matmuls(rows):
        mix = (jnp.dot(a_ref[rows, :], wo_ref[0:ATTN_WIDTH, :], preferred_element_type=F32)
               + jnp.dot(r_ref[rows, :], wo_ref[ATTN_WIDTH:, :], preferred_element_type=F32))
        u = DEEPNORM_ALPHA * x_ref[rows, :] + mix
        ple = jnp.dot(p_ref[rows, :].astype(BF16), wp_ref[...], preferred_element_type=F32)
        gate_in = jnp.dot(u.astype(BF16), wg_ref[...], preferred_element_type=F32)
        return u, ple, gate_in

    def tail(rows, u, ple, gate_in):
        u = u + jax.nn.sigmoid(gate_in) * ple
        mu = jnp.mean(u, axis=-1, keepdims=True)
        cen = u - mu
        var = jnp.mean(cen * cen, axis=-1, keepdims=True)
        o_ref[rows, :] = cen * lax.rsqrt(var + LN_EPS) * gain_ref[...] + bias_ref[...]

    n_sub = o_ref.shape[0] // sub
    rows = [slice(r * sub, (r + 1) * sub) for r in range(n_sub)]
    pending = matmuls(rows[0])
    for r in range(n_sub):
        following = matmuls(rows[r + 1]) if r + 1 < n_sub else None
        tail(rows[r], *pending)
        pending = following


def _output_stage(x2d, attn2d, ret2d, p2d, w_out, w_gate, w_proj, ln_gain, ln_bias, *, tm=1024, sub=256):
    m, dm = x2d.shape
    rows = lambda width: pl.BlockSpec((tm, width), lambda i: (i, 0))
    return pl.pallas_call(
        functools.partial(_output_kernel, sub=sub),
        grid=(m // tm,),
        in_specs=[rows(dm), rows(ATTN_WIDTH), rows(RET_WIDTH), rows(D_PLE),
                  _resident(w_out.shape), _resident(w_gate.shape), _resident(w_proj.shape),
                  _resident((1, dm)), _resident((1, dm))],
        out_specs=rows(dm),
        out_shape=jax.ShapeDtypeStruct((m, dm), F32),
        scratch_shapes=[pltpu.VMEM(w_out.shape, BF16), pltpu.VMEM(w_gate.shape, BF16),
                        pltpu.VMEM(w_proj.shape, BF16)],
        compiler_params=pltpu.CompilerParams(dimension_semantics=("arbitrary",),
                                             vmem_limit_bytes=VMEM_LIMIT),
        name="output_stage",
    )(x2d, attn2d, ret2d, p2d, w_out, w_gate, w_proj,
      ln_gain.astype(F32).reshape(1, dm), ln_bias.astype(F32).reshape(1, dm))


def kernel(x, p, w_in, w_out, ret_gn_gain, w_ple_gate, w_ple_proj, ln_gain, ln_bias):
    b, s, dm = x.shape
    h = x
    for i in range(DEPTH):
        h2d = h.reshape(b * s, dm)
        z = _input_projection(h2d, w_in[i].astype(F32), s)
        z3d = z.reshape(b, s, IN_WIDTH)
        attn = _moba_attention(z3d)
        ret = _retention(z3d, ret_gn_gain[i])
        out = _output_stage(h2d, attn.reshape(b * s, ATTN_WIDTH), ret.reshape(b * s, RET_WIDTH),
                            p[i].reshape(b * s, D_PLE), w_out[i].astype(F32), w_ple_gate[i].astype(F32),
                            w_ple_proj[i].astype(F32), ln_gain[i], ln_bias[i])
        h = out.reshape(b, s, dm)
    return h
```

```python
import functools

import jax
import jax.numpy as jnp
import numpy as np
from jax import lax
from jax.experimental import pallas as pl
from jax.experimental.pallas import tpu as pltpu

F32 = jnp.float32
BF16 = jnp.bfloat16

D_MODEL = 1024
D_PLE = 256
ATTN_HEADS = 8
ATTN_HEAD_DIM = 64
ATTN_WIDTH = ATTN_HEADS * ATTN_HEAD_DIM
RET_HEADS = 4
RET_HEAD_DIM = 128
RET_WIDTH = RET_HEADS * RET_HEAD_DIM
IN_WIDTH = 4 * ATTN_WIDTH + 4 * RET_WIDTH
MOBA_BLOCK = 256
MOBA_TOPK = 3
ROPE_BASE = 10000.0
DEPTH = 1
DEEPNORM_ALPHA = (2.0 * DEPTH) ** 0.25
LN_EPS = 1e-5
GN_EPS = 1e-6

LANES = 128
HEADS_PER_STEP = LANES // ATTN_HEAD_DIM
MASKED = -1e30
LOG2_E = 1.4426950408889634
ONES_ROWS = 16
CHUNK_BLOCKS = 4
RET_CHUNK = 256
VMEM_LIMIT = 48 * 1024 * 1024
OUTPUT_VMEM_LIMIT = 58 * 1024 * 1024


def _proj_kernel(x_ref, w32_ref, cos_ref, sin_ref, o_ref, w_ref, *, n_chunk):
    @pl.when(pl.program_id(0) == 0)
    def _cast_weights_once():
        for lo in range(0, w_ref.shape[1], n_chunk):
            w_ref[:, lo:lo + n_chunk] = w32_ref[:, lo:lo + n_chunk].astype(BF16)

    xb = x_ref[...].astype(BF16)
    d = RET_HEAD_DIM
    rot_lo = 4 * ATTN_WIDTH
    rot_mid = rot_lo + RET_WIDTH
    rot_hi = rot_mid + RET_WIDTH
    assert rot_lo % n_chunk == 0 and rot_hi % n_chunk == 0 and n_chunk % d == 0
    for lo in range(0, o_ref.shape[1], n_chunk):
        acc = jnp.dot(xb, w_ref[:, lo:lo + n_chunk], preferred_element_type=F32)
        if rot_lo <= lo < rot_hi:
            cos = cos_ref[...]
            sin = sin_ref[...]
            for col in range(lo, lo + n_chunk, d):
                head = acc[:, col - lo:col - lo + d]
                rotated = head * cos + pltpu.roll(head, d // 2, 1) * sin
                if col >= rot_mid:
                    rotated = rotated * (d ** -0.5)
                o_ref[:, col:col + d] = rotated.astype(o_ref.dtype)
        else:
            o_ref[:, lo:lo + n_chunk] = acc.astype(o_ref.dtype)


def _rotary_tables(s):
    half = RET_HEAD_DIM // 2
    freqs = ROPE_BASE ** (-np.arange(half, dtype=np.float64) / half)
    ang = np.arange(s, dtype=np.float64)[:, None] * freqs[None, :]
    cos, sin = np.cos(ang), np.sin(ang)
    return (np.concatenate([cos, cos], axis=-1).astype(np.float32),
            np.concatenate([-sin, sin], axis=-1).astype(np.float32))


def _resident(shape):
    return pl.BlockSpec(shape, lambda *_: (0,) * len(shape), pipeline_mode=pl.Buffered(1))


def _input_projection(x2d, w_in, seq, *, tm=512, n_chunk=1024):
    m, k = x2d.shape
    n = w_in.shape[1]
    cos, sin = _rotary_tables(seq)
    pos = pl.BlockSpec((tm, RET_HEAD_DIM), lambda i: (i % (seq // tm), 0))
    return pl.pallas_call(
        functools.partial(_proj_kernel, n_chunk=n_chunk),
        grid=(m // tm,),
        in_specs=[pl.BlockSpec((tm, k), lambda i: (i, 0)), _resident((k, n)), pos, pos],
        out_specs=pl.BlockSpec((tm, n), lambda i: (i, 0)),
        out_shape=jax.ShapeDtypeStruct((m, n), BF16),
        scratch_shapes=[pltpu.VMEM((k, n), BF16)],
        compiler_params=pltpu.CompilerParams(dimension_semantics=("arbitrary",),
                                             vmem_limit_bytes=VMEM_LIMIT),
        name="input_projection",
    )(x2d, w_in, cos, sin)


def _moba_kernel(q_ref, k_ref, v_ref, g_ref, onehot_ref, o_ref,
                 vtc_ref, qtb_ref, s0_ref, s1_ref, mrun_ref, mnow_ref, alpha_ref, acc_ref, *, nb):
    blk = MOBA_BLOCK
    hd = ATTN_HEAD_DIM
    nh = HEADS_PER_STEP
    width = nh * blk
    vrows = LANES + ONES_ROWS
    seq = nb * blk
    chunk = CHUNK_BLOCKS * blk
    n_chunks = nb // CHUNK_BLOCKS
    s_refs = (s0_ref, s1_ref)

    @pl.when((pl.program_id(0) == 0) & (pl.program_id(1) == 0))
    def _fill_constant_regions():
        vtc_ref[:, LANES:vrows, :] = jnp.ones((n_chunks, ONES_ROWS, chunk), BF16)
        qtb_ref[...] = jnp.zeros(qtb_ref.shape, BF16)

    def _prepare_head_pair():
        for n in range(nb):
            c, j = divmod(n, CHUNK_BLOCKS)
            vtc_ref[c, 0:LANES, j * blk:(j + 1) * blk] = (
                v_ref[0, n * blk:(n + 1) * blk, :].astype(F32).T.astype(BF16))
        key_blk = lax.broadcasted_iota(jnp.int32, (nb, seq), 1) // blk
        row_blk = lax.broadcasted_iota(jnp.int32, (nb, seq), 0)
        averager = jnp.where(key_blk == row_blk, 1.0 / blk, 0.0).astype(BF16)
        kmean = jnp.dot(averager, k_ref[0], preferred_element_type=F32)
        hi = kmean.astype(BF16)
        lo = (kmean - hi.astype(F32)).astype(BF16)
        lane = lax.broadcasted_iota(jnp.int32, (nb, LANES), 1)
        parts = []
        for h in range(nh):
            in_head = (lane >= h * hd) & (lane < (h + 1) * hd)
            parts += [jnp.where(in_head, hi, jnp.zeros_like(hi)), jnp.where(in_head, lo, jnp.zeros_like(lo))]
        kstack = jnp.concatenate(parts, axis=0)
        qts = []
        for n in range(nb):
            qt = (q_ref[0, n * blk:(n + 1) * blk, :].astype(F32) * (hd ** -0.5 * LOG2_E)).T.astype(BF16)
            for h in range(nh):
                qtb_ref[n, h * hd:(h + 1) * hd, h * blk:(h + 1) * blk] = qt[h * hd:(h + 1) * hd]
            qts.append(qt)
        gates = jnp.dot(kstack, jnp.concatenate(qts, axis=1), preferred_element_type=F32)
        past = row_blk < key_blk
        row_f = row_blk.astype(F32)
        own_f = jnp.where(row_blk == key_blk, 1.0, 0.0)
        for h in range(nh):
            g = gates[(2 * h) * nb:(2 * h + 1) * nb] + gates[(2 * h + 1) * nb:(2 * h + 2) * nb]
            g = jnp.where(past, g, -jnp.inf)
            picked = jnp.zeros((nb, seq), F32)
            for _ in range(MOBA_TOPK):
                top = jnp.max(g, axis=0, keepdims=True)
                first = jnp.min(jnp.where(g == top, row_f, float(nb)), axis=0, keepdims=True)
                pick = row_f == first
                picked = jnp.where(pick, 1.0, picked)
                g = jnp.where(pick, -jnp.inf, g)
            bias = jnp.where(jnp.where(past, picked, own_f) > 0.5, 0.0, MASKED).astype(BF16)
            for n in range(nb):
                qtb_ref[n, LANES:LANES + nb, h * blk:(h + 1) * blk] = bias[:, n * blk:(n + 1) * blk]

    _prepare_head_pair()

    key_pos = lax.broadcasted_iota(jnp.int32, (blk, width), 0)
    q_pos = lax.broadcasted_iota(jnp.int32, (blk, width), 1) & (blk - 1)
    causal = key_pos <= q_pos

    def rows_at(start, size):
        if isinstance(start, int):
            return pl.ds(start, size)
        return pl.ds(pl.multiple_of(start, blk), size)

    def scores(c, qi, slot, cnt, own_j=None):
        key_rows = rows_at(c * chunk, cnt * blk)
        keys_ext = jnp.concatenate([k_ref[0, key_rows, :], onehot_ref[key_rows, :]], axis=1)
        s = jnp.dot(keys_ext, qtb_ref[qi], preferred_element_type=F32)
        m8 = None
        for j in range(cnt):
            sj = s[j * blk:(j + 1) * blk]
            if j == own_j:
                sj = jnp.where(causal, sj, MASKED)
            s_refs[slot][j * blk:(j + 1) * blk, :] = sj
            mj = jnp.max(sj.reshape(blk // 8, 8, width), axis=0)
            m8 = mj if m8 is None else jnp.maximum(m8, mj)
        m_new = jnp.broadcast_to(jnp.max(m8, axis=0, keepdims=True), (8, width))
        if c > 0:
            m_old = mrun_ref[qi]
            m_new = jnp.maximum(m_old, m_new)
            alpha_ref[slot] = jnp.exp2(m_old - m_new)
        mrun_ref[qi] = m_new
        mnow_ref[slot] = m_new

    def values(c, qi, slot, cnt):
        p = jnp.exp2(s_refs[slot][0:cnt * blk, :] - mnow_ref[slot, 0:1, :]).astype(BF16)
        pv = jnp.dot(vtc_ref[c, :, 0:cnt * blk], p, preferred_element_type=F32)
        if c > 0:
            pv = acc_ref[qi] * alpha_ref[slot, 0:1, :] + pv
        acc_ref[qi] = pv

    def finish(qi):
        outs = []
        for h in range(nh):
            cols = slice(h * blk, (h + 1) * blk)
            outs.append(acc_ref[qi, h * hd:(h + 1) * hd, cols] / acc_ref[qi, LANES:LANES + 1, cols])
        attn = jnp.concatenate(outs, axis=0).T
        gate = g_ref[0, rows_at(qi * blk, blk), :].astype(F32)
        o_ref[0, rows_at(qi * blk, blk), :] = (attn * (gate * jax.nn.sigmoid(gate))).astype(o_ref.dtype)

    assert CHUNK_BLOCKS % 2 == 0 and nb % CHUNK_BLOCKS == 0

    def chunk_section(c):
        first = c * CHUNK_BLOCKS
        later = first + CHUNK_BLOCKS
        scores(c, first, 0, 1, own_j=0)
        for j in range(CHUNK_BLOCKS):
            if j + 1 < CHUNK_BLOCKS:
                scores(c, first + j + 1, (j + 1) % 2, j + 2, own_j=j + 1)
            elif later < nb:
                scores(c, later, 0, CHUNK_BLOCKS)
            values(c, first + j, j % 2, j + 1)
            finish(first + j)
        for qi in range(later, nb):
            if qi + 1 < nb:
                scores(c, qi + 1, (qi + 1 - later) % 2, CHUNK_BLOCKS)
            values(c, qi, (qi - later) % 2, CHUNK_BLOCKS)

    chunk_section(0)

    @pl.when(pl.program_id(1) >= 0)
    def _remaining_chunks():
        for c in range(1, n_chunks):
            chunk_section(c)


def _moba_attention(z3d):
    b, s, _ = z3d.shape
    nb = s // MOBA_BLOCK
    steps = ATTN_WIDTH // LANES
    width = HEADS_PER_STEP * MOBA_BLOCK
    vrows = LANES + ONES_ROWS
    kernel = functools.partial(_moba_kernel, nb=nb)
    full = lambda off: pl.BlockSpec((1, s, LANES), lambda bi, hp: (bi, 0, off * steps + hp))
    onehot = jnp.asarray(np.arange(s)[:, None] // MOBA_BLOCK == np.arange(LANES)[None, :], dtype=BF16)
    return pl.pallas_call(
        kernel,
        grid=(b, steps),
        in_specs=[full(0), full(1), full(2), full(3), _resident((s, LANES))],
        out_specs=pl.BlockSpec((1, s, LANES), lambda bi, hp: (bi, 0, hp)),
        out_shape=jax.ShapeDtypeStruct((b, s, ATTN_WIDTH), BF16),
        scratch_shapes=[pltpu.VMEM((nb // CHUNK_BLOCKS, vrows, CHUNK_BLOCKS * MOBA_BLOCK), BF16),
                        pltpu.VMEM((nb, 2 * LANES, width), BF16),
                        pltpu.VMEM((CHUNK_BLOCKS * MOBA_BLOCK, width), F32),
                        pltpu.VMEM((CHUNK_BLOCKS * MOBA_BLOCK, width), F32),
                        pltpu.VMEM((nb, 8, width), F32),
                        pltpu.VMEM((2, 8, width), F32),
                        pltpu.VMEM((2, 8, width), F32),
                        pltpu.VMEM((nb, vrows, width), F32)],
        compiler_params=pltpu.CompilerParams(dimension_semantics=("arbitrary",) * 2,
                                             vmem_limit_bytes=VMEM_LIMIT),
        name="moba_attention",
    )(z3d, z3d, z3d, z3d, onehot)


def _retention_kernel(q_ref, k_ref, v_ref, g_ref, decay_ref, xi_ref, zeta_ref, gc_ref, gain_ref,
                      o_ref, state_ref):
    c = pl.program_id(0)
    d = RET_HEAD_DIM

    @pl.when(c == 0)
    def _reset_state():
        state_ref[...] = jnp.zeros_like(state_ref)

    for bi in range(q_ref.shape[0]):
        for h in range(RET_HEADS):
            cols = slice(h * d, (h + 1) * d)
            qb = q_ref[bi, :, cols]
            kb = k_ref[bi, :, cols]
            vb = v_ref[bi, :, cols]
            scores = lax.dot_general(qb, kb, (((1,), (1,)), ((), ())), preferred_element_type=F32)
            inner = (scores * decay_ref[h]).astype(BF16)
            state = state_ref[bi, h]
            out = (jnp.dot(inner, vb, preferred_element_type=F32)
                   + jnp.dot(qb, state.astype(BF16), preferred_element_type=F32) * xi_ref[h])
            vz = (vb.astype(F32) * zeta_ref[h]).astype(BF16)
            state_ref[bi, h] = state * gc_ref[h] + lax.dot_general(
                kb, vz, (((0,), (0,)), ((), ())), preferred_element_type=F32)

            mu = jnp.mean(out, axis=-1, keepdims=True)
            cen = out - mu
            var = jnp.mean(cen * cen, axis=-1, keepdims=True)
            normed = cen * lax.rsqrt(var + GN_EPS) * gain_ref[h]
            gate = g_ref[bi, :, cols].astype(F32)
            o_ref[bi, :, cols] = (normed * (gate * jax.nn.sigmoid(gate))).astype(o_ref.dtype)


def _retention_constants():
    d, c = RET_HEAD_DIM, RET_CHUNK
    log_g = np.log1p(-np.exp2(-5.0 - np.arange(RET_HEADS, dtype=np.float64)))
    idx = np.arange(c, dtype=np.float64)
    rel = idx[:, None] - idx[None, :]
    decay = np.where(rel >= 0, np.exp(np.maximum(rel, 0.0)[None] * log_g[:, None, None]), 0.0)
    xi = np.exp((idx + 1.0)[None] * log_g[:, None])
    zeta = np.exp((c - 1.0 - idx)[None] * log_g[:, None])
    g_chunk = np.exp(c * log_g)
    bcast = lambda t: np.ascontiguousarray(np.broadcast_to(t[..., None], t.shape + (d,)), dtype=np.float32)
    return decay.astype(np.float32), bcast(xi), bcast(zeta), bcast(g_chunk[:, None])


def _retention(z3d, ret_gn_gain):
    b, s, _ = z3d.shape
    c, d = RET_CHUNK, RET_HEAD_DIM
    base = 4 * ATTN_WIDTH // RET_WIDTH
    decay, xi, zeta, gc = _retention_constants()
    gain = ret_gn_gain.astype(F32).reshape(RET_HEADS, 1, d)
    zspec = lambda off: pl.BlockSpec((b, c, RET_WIDTH), lambda ci: (0, ci, base + off))
    whole = lambda *shape: pl.BlockSpec(shape, lambda ci: (0,) * len(shape))
    return pl.pallas_call(
        _retention_kernel,
        grid=(s // c,),
        in_specs=[zspec(0), zspec(1), zspec(2), zspec(3),
                  whole(RET_HEADS, c, c), whole(RET_HEADS, c, d), whole(RET_HEADS, c, d),
                  whole(RET_HEADS, 1, d), whole(RET_HEADS, 1, d)],
        out_specs=pl.BlockSpec((b, c, RET_WIDTH), lambda ci: (0, ci, 0)),
        out_shape=jax.ShapeDtypeStruct((b, s, RET_WIDTH), BF16),
        scratch_shapes=[pltpu.VMEM((b, RET_HEADS, d, d), F32)],
        compiler_params=pltpu.CompilerParams(dimension_semantics=("arbitrary",),
                                             vmem_limit_bytes=VMEM_LIMIT),
        name="retention",
    )(z3d, z3d, z3d, z3d, decay, xi, zeta, gc, gain)


def _output_kernel(x_ref, a_ref, r_ref, p_ref, wo_hbm, wg_hbm, wp32_ref, gain_ref, bias_ref, o_ref,
                   wo_ref, wg_ref, wp_ref, w_sem, *, sub):
    @pl.when(pl.program_id(0) == 0)
    def _fetch_and_cast_weights_once():
        dm = wo_ref.shape[0]
        assert wo_ref.shape == wg_ref.shape == (dm, dm) and o_ref.shape[0] >= 2 * dm
        wo_copy = pltpu.make_async_copy(wo_hbm, o_ref.at[pl.ds(0, dm), :], w_sem.at[0])
        wg_copy = pltpu.make_async_copy(wg_hbm, o_ref.at[pl.ds(dm, dm), :], w_sem.at[1])
        wo_copy.start()
        wg_copy.start()
        wp_ref[...] = wp32_ref[...].astype(BF16)
        wo_copy.wait()
        wo_ref[...] = o_ref[0:dm, :].astype(BF16)
        wg_copy.wait()
        wg_ref[...] = o_ref[dm:2 * dm, :].astype(BF16)

    def matmuls(rows):
        mix = (jnp.dot(a_ref[rows, :], wo_ref[0:ATTN_WIDTH, :], preferred_element_type=F32)
               + jnp.dot(r_ref[rows, :], wo_ref[ATTN_WIDTH:, :], preferred_element_type=F32))
        u = DEEPNORM_ALPHA * x_ref[rows, :] + mix
        ple = jnp.dot(p_ref[rows, :].astype(BF16), wp_ref[...], preferred_element_type=F32)
        gate_in = jnp.dot(u.astype(BF16), wg_ref[...], preferred_element_type=F32)
        return u, ple, gate_in

    def tail(rows, u, ple, gate_in):
        u = u + jax.nn.sigmoid(gate_in) * ple
        mu = jnp.mean(u, axis=-1, keepdims=True)
        cen = u - mu
        var = jnp.mean(cen * cen, axis=-1, keepdims=True)
        o_ref[rows, :] = cen * lax.rsqrt(var + LN_EPS) * gain_ref[...] + bias_ref[...]

    n_sub = o_ref.shape[0] // sub
    rows = [slice(r * sub, (r + 1) * sub) for r in range(n_sub)]
    pending = matmuls(rows[0])
    for r in range(n_sub):
        following = matmuls(rows[r + 1]) if r + 1 < n_sub else None
        tail(rows[r], *pending)
        pending = following


def _output_stage(x2d, attn2d, ret2d, p2d, w_out, w_gate, w_proj, ln_gain, ln_bias, *, tm=2048, sub=256):
    m, dm = x2d.shape
    rows = lambda width: pl.BlockSpec((tm, width), lambda i: (i, 0))
    return pl.pallas_call(
        functools.partial(_output_kernel, sub=sub),
        grid=(m // tm,),
        in_specs=[rows(dm), rows(ATTN_WIDTH), rows(RET_WIDTH), rows(D_PLE),
                  pl.BlockSpec(memory_space=pl.ANY), pl.BlockSpec(memory_space=pl.ANY), _resident(w_proj.shape),
                  _resident((1, dm)), _resident((1, dm))],
        out_specs=rows(dm),
        out_shape=jax.ShapeDtypeStruct((m, dm), F32),
        scratch_shapes=[pltpu.VMEM(w_out.shape, BF16), pltpu.VMEM(w_gate.shape, BF16),
                        pltpu.VMEM(w_proj.shape, BF16), pltpu.SemaphoreType.DMA((2,))],
        compiler_params=pltpu.CompilerParams(dimension_semantics=("arbitrary",),
                                             vmem_limit_bytes=OUTPUT_VMEM_LIMIT),
        name="output_stage",
    )(x2d, attn2d, ret2d, p2d, w_out, w_gate, w_proj,
      ln_gain.astype(F32).reshape(1, dm), ln_bias.astype(F32).reshape(1, dm))


def kernel(x, p, w_in, w_out, ret_gn_gain, w_ple_gate, w_ple_proj, ln_gain, ln_bias):
    b, s, dm = x.shape
    h = x
    for i in range(DEPTH):
        h2d = h.reshape(b * s, dm)
        z = _input_projection(h2d, w_in[i].astype(F32), s)
        z3d = z.reshape(b, s, IN_WIDTH)
        attn = _moba_attention(z3d)
        ret = _retention(z3d, ret_gn_gain[i])
        out = _output_stage(h2d, attn.reshape(b * s, ATTN_WIDTH), ret.reshape(b * s, RET_WIDTH),
                            p[i].reshape(b * s, D_PLE), w_out[i].astype(F32), w_ple_gate[i].astype(F32),
                            w_ple_proj[i].astype(F32), ln_gain[i], ln_bias[i])
        h = out.reshape(b, s, dm)
    return h
```

```python
import functools

import jax
import jax.numpy as jnp
import numpy as np
from jax import lax
from jax.experimental import pallas as pl
from jax.experimental.pallas import tpu as pltpu

F32 = jnp.float32
BF16 = jnp.bfloat16

D_MODEL = 1024
D_PLE = 256
ATTN_HEADS = 8
ATTN_HEAD_DIM = 64
ATTN_WIDTH = ATTN_HEADS * ATTN_HEAD_DIM
RET_HEADS = 4
RET_HEAD_DIM = 128
RET_WIDTH = RET_HEADS * RET_HEAD_DIM
IN_WIDTH = 4 * ATTN_WIDTH + 4 * RET_WIDTH
MOBA_BLOCK = 256
MOBA_TOPK = 3
ROPE_BASE = 10000.0
DEPTH = 1
DEEPNORM_ALPHA = (2.0 * DEPTH) ** 0.25
LN_EPS = 1e-5
GN_EPS = 1e-6

LANES = 128
HEADS_PER_STEP = LANES // ATTN_HEAD_DIM
MASKED = -1e30
LOG2_E = 1.4426950408889634
ONES_ROWS = 16
CHUNK_BLOCKS = 4
RET_CHUNK = 256
VMEM_LIMIT = 48 * 1024 * 1024


def _proj_kernel(x_ref, w32_ref, cos_ref, sin_ref, o_ref, w_ref, *, n_chunk):
    @pl.when(pl.program_id(0) == 0)
    def _cast_weights_once():
        for lo in range(0, w_ref.shape[1], n_chunk):
            w_ref[:, lo:lo + n_chunk] = w32_ref[:, lo:lo + n_chunk].astype(BF16)

    xb = x_ref[...].astype(BF16)
    d = RET_HEAD_DIM
    rot_lo = 4 * ATTN_WIDTH
    rot_mid = rot_lo + RET_WIDTH
    rot_hi = rot_mid + RET_WIDTH
    assert rot_lo % n_chunk == 0 and rot_hi % n_chunk == 0 and n_chunk % d == 0
    for lo in range(0, o_ref.shape[1], n_chunk):
        acc = jnp.dot(xb, w_ref[:, lo:lo + n_chunk], preferred_element_type=F32)
        if rot_lo <= lo < rot_hi:
            cos = cos_ref[...]
            sin = sin_ref[...]
            for col in range(lo, lo + n_chunk, d):
                head = acc[:, col - lo:col - lo + d]
                rotated = head * cos + pltpu.roll(head, d // 2, 1) * sin
                if col >= rot_mid:
                    rotated = rotated * (d ** -0.5)
                o_ref[:, col:col + d] = rotated.astype(o_ref.dtype)
        else:
            o_ref[:, lo:lo + n_chunk] = acc.astype(o_ref.dtype)


def _rotary_tables(s):
    half = RET_HEAD_DIM // 2
    freqs = ROPE_BASE ** (-np.arange(half, dtype=np.float64) / half)
    ang = np.arange(s, dtype=np.float64)[:, None] * freqs[None, :]
    cos, sin = np.cos(ang), np.sin(ang)
    return (np.concatenate([cos, cos], axis=-1).astype(np.float32),
            np.concatenate([-sin, sin], axis=-1).astype(np.float32))


def _resident(shape):
    return pl.BlockSpec(shape, lambda *_: (0,) * len(shape), pipeline_mode=pl.Buffered(1))


def _input_projection(x2d, w_in, seq, *, tm=512, n_chunk=1024):
    m, k = x2d.shape
    n = w_in.shape[1]
    cos, sin = _rotary_tables(seq)
    pos = pl.BlockSpec((tm, RET_HEAD_DIM), lambda i: (i % (seq // tm), 0))
    return pl.pallas_call(
        functools.partial(_proj_kernel, n_chunk=n_chunk),
        grid=(m // tm,),
        in_specs=[pl.BlockSpec((tm, k), lambda i: (i, 0)), _resident((k, n)), pos, pos],
        out_specs=pl.BlockSpec((tm, n), lambda i: (i, 0)),
        out_shape=jax.ShapeDtypeStruct((m, n), BF16),
        scratch_shapes=[pltpu.VMEM((k, n), BF16)],
        compiler_params=pltpu.CompilerParams(dimension_semantics=("arbitrary",),
                                             vmem_limit_bytes=VMEM_LIMIT),
        name="input_projection",
    )(x2d, w_in, cos, sin)


def _moba_kernel(q_ref, k_ref, v_ref, g_ref, onehot_ref, o_ref,
                 vtc_ref, qtb_ref, s0_ref, s1_ref, mrun_ref, mnow_ref, alpha_ref, acc_ref, *, nb):
    blk = MOBA_BLOCK
    hd = ATTN_HEAD_DIM
    nh = HEADS_PER_STEP
    width = nh * blk
    vrows = LANES + ONES_ROWS
    seq = nb * blk
    chunk = CHUNK_BLOCKS * blk
    n_chunks = nb // CHUNK_BLOCKS
    s_refs = (s0_ref, s1_ref)

    @pl.when((pl.program_id(0) == 0) & (pl.program_id(1) == 0))
    def _fill_constant_regions():
        vtc_ref[:, LANES:vrows, :] = jnp.ones((n_chunks, ONES_ROWS, chunk), BF16)
        qtb_ref[...] = jnp.zeros(qtb_ref.shape, BF16)

    def _prepare_head_pair():
        for n in range(nb):
            c, j = divmod(n, CHUNK_BLOCKS)
            vtc_ref[c, 0:LANES, j * blk:(j + 1) * blk] = (
                v_ref[0, n * blk:(n + 1) * blk, :].astype(F32).T.astype(BF16))
        key_blk = lax.broadcasted_iota(jnp.int32, (nb, seq), 1) // blk
        row_blk = lax.broadcasted_iota(jnp.int32, (nb, seq), 0)
        averager = jnp.where(key_blk == row_blk, 1.0 / blk, 0.0).astype(BF16)
        kmean = jnp.dot(averager, k_ref[0], preferred_element_type=F32)
        hi = kmean.astype(BF16)
        lo = (kmean - hi.astype(F32)).astype(BF16)
        lane = lax.broadcasted_iota(jnp.int32, (nb, LANES), 1)
        parts = []
        for h in range(nh):
            in_head = (lane >= h * hd) & (lane < (h + 1) * hd)
            parts += [jnp.where(in_head, hi, jnp.zeros_like(hi)), jnp.where(in_head, lo, jnp.zeros_like(lo))]
        kstack = jnp.concatenate(parts, axis=0)
        qts = []
        for n in range(nb):
            qt = (q_ref[0, n * blk:(n + 1) * blk, :].astype(F32) * (hd ** -0.5 * LOG2_E)).T.astype(BF16)
            for h in range(nh):
                qtb_ref[n, h * hd:(h + 1) * hd, h * blk:(h + 1) * blk] = qt[h * hd:(h + 1) * hd]
            qts.append(qt)
        gates = jnp.dot(kstack, jnp.concatenate(qts, axis=1), preferred_element_type=F32)
        past = row_blk < key_blk
        row_f = row_blk.astype(F32)
        own_f = jnp.where(row_blk == key_blk, 1.0, 0.0)
        for h in range(nh):
            g = gates[(2 * h) * nb:(2 * h + 1) * nb] + gates[(2 * h + 1) * nb:(2 * h + 2) * nb]
            g = jnp.where(past, g, -jnp.inf)
            picked = jnp.zeros((nb, seq), F32)
            for _ in range(MOBA_TOPK):
                top = jnp.max(g, axis=0, keepdims=True)
                first = jnp.min(jnp.where(g == top, row_f, float(nb)), axis=0, keepdims=True)
                pick = row_f == first
                picked = jnp.where(pick, 1.0, picked)
                g = jnp.where(pick, -jnp.inf, g)
            bias = jnp.where(jnp.where(past, picked, own_f) > 0.5, 0.0, MASKED).astype(BF16)
            for n in range(nb):
                qtb_ref[n, LANES:LANES + nb, h * blk:(h + 1) * blk] = bias[:, n * blk:(n + 1) * blk]

    _prepare_head_pair()

    key_pos = lax.broadcasted_iota(jnp.int32, (blk, width), 0)
    q_pos = lax.broadcasted_iota(jnp.int32, (blk, width), 1) & (blk - 1)
    causal = key_pos <= q_pos

    def rows_at(start, size):
        if isinstance(start, int):
            return pl.ds(start, size)
        return pl.ds(pl.multiple_of(start, blk), size)

    def scores(c, qi, slot, cnt, own_j=None):
        key_rows = rows_at(c * chunk, cnt * blk)
        keys_ext = jnp.concatenate([k_ref[0, key_rows, :], onehot_ref[key_rows, :]], axis=1)
        s = jnp.dot(keys_ext, qtb_ref[qi], preferred_element_type=F32)
        m8 = None
        for j in range(cnt):
            sj = s[j * blk:(j + 1) * blk]
            if j == own_j:
                sj = jnp.where(causal, sj, MASKED)
            s_refs[slot][j * blk:(j + 1) * blk, :] = sj
            mj = jnp.max(sj.reshape(blk // 8, 8, width), axis=0)
            m8 = mj if m8 is None else jnp.maximum(m8, mj)
        m_new = jnp.broadcast_to(jnp.max(m8, axis=0, keepdims=True), (8, width))
        if c > 0:
            m_old = mrun_ref[qi]
            m_new = jnp.maximum(m_old, m_new)
            alpha_ref[slot] = jnp.exp2(m_old - m_new)
        mrun_ref[qi] = m_new
        mnow_ref[slot] = m_new

    def values(c, qi, slot, cnt):
        p = jnp.exp2(s_refs[slot][0:cnt * blk, :] - mnow_ref[slot, 0:1, :]).astype(BF16)
        pv = jnp.dot(vtc_ref[c, :, 0:cnt * blk], p, preferred_element_type=F32)
        if c > 0:
            pv = acc_ref[qi] * alpha_ref[slot, 0:1, :] + pv
        acc_ref[qi] = pv

    def finish(qi):
        outs = []
        for h in range(nh):
            cols = slice(h * blk, (h + 1) * blk)
            outs.append(acc_ref[qi, h * hd:(h + 1) * hd, cols] / acc_ref[qi, LANES:LANES + 1, cols])
        attn = jnp.concatenate(outs, axis=0).T
        gate = g_ref[0, rows_at(qi * blk, blk), :].astype(F32)
        o_ref[0, rows_at(qi * blk, blk), :] = (attn * (gate * jax.nn.sigmoid(gate))).astype(o_ref.dtype)

    assert CHUNK_BLOCKS % 2 == 0 and nb % CHUNK_BLOCKS == 0

    def chunk_section(c):
        first = c * CHUNK_BLOCKS
        later = first + CHUNK_BLOCKS
        scores(c, first, 0, 1, own_j=0)
        for j in range(CHUNK_BLOCKS):
            if j + 1 < CHUNK_BLOCKS:
                scores(c, first + j + 1, (j + 1) % 2, j + 2, own_j=j + 1)
            elif later < nb:
                scores(c, later, 0, CHUNK_BLOCKS)
            values(c, first + j, j % 2, j + 1)
            finish(first + j)
        for qi in range(later, nb):
            if qi + 1 < nb:
                scores(c, qi + 1, (qi + 1 - later) % 2, CHUNK_BLOCKS)
            values(c, qi, (qi - later) % 2, CHUNK_BLOCKS)

    chunk_section(0)

    @pl.when(pl.program_id(1) >= 0)
    def _remaining_chunks():
        for c in range(1, n_chunks):
            chunk_section(c)


def _moba_attention(z3d):
    b, s, _ = z3d.shape
    nb = s // MOBA_BLOCK
    steps = ATTN_WIDTH // LANES
    width = HEADS_PER_STEP * MOBA_BLOCK
    vrows = LANES + ONES_ROWS
    kernel = functools.partial(_moba_kernel, nb=nb)
    full = lambda off: pl.BlockSpec((1, s, LANES), lambda bi, hp: (bi, 0, off * steps + hp))
    onehot = jnp.asarray(np.arange(s)[:, None] // MOBA_BLOCK == np.arange(LANES)[None, :], dtype=BF16)
    return pl.pallas_call(
        kernel,
        grid=(b, steps),
        in_specs=[full(0), full(1), full(2), full(3), _resident((s, LANES))],
        out_specs=pl.BlockSpec((1, s, LANES), lambda bi, hp: (bi, 0, hp)),
        out_shape=jax.ShapeDtypeStruct((b, s, ATTN_WIDTH), BF16),
        scratch_shapes=[pltpu.VMEM((nb // CHUNK_BLOCKS, vrows, CHUNK_BLOCKS * MOBA_BLOCK), BF16),
                        pltpu.VMEM((nb, 2 * LANES, width), BF16),
                        pltpu.VMEM((CHUNK_BLOCKS * MOBA_BLOCK, width), F32),
                        pltpu.VMEM((CHUNK_BLOCKS * MOBA_BLOCK, width), F32),
                        pltpu.VMEM((nb, 8, width), F32),
                        pltpu.VMEM((2, 8, width), F32),
                        pltpu.VMEM((2, 8, width), F32),
                        pltpu.VMEM((nb, vrows, width), F32)],
        compiler_params=pltpu.CompilerParams(dimension_semantics=("arbitrary",) * 2,
                                             vmem_limit_bytes=VMEM_LIMIT),
        name="moba_attention",
    )(z3d, z3d, z3d, z3d, onehot)


def _retention_kernel(q_ref, k_ref, v_ref, g_ref, decay_ref, xi_ref, zeta_ref, gc_ref, gain_ref,
                      o_ref, state_ref):
    c = pl.program_id(0)
    d = RET_HEAD_DIM

    @pl.when(c == 0)
    def _reset_state():
        state_ref[...] = jnp.zeros_like(state_ref)

    for bi in range(q_ref.shape[0]):
        for h in range(RET_HEADS):
            cols = slice(h * d, (h + 1) * d)
            qb = q_ref[bi, :, cols]
            kb = k_ref[bi, :, cols]
            vb = v_ref[bi, :, cols]
            scores = lax.dot_general(qb, kb, (((1,), (1,)), ((), ())), preferred_element_type=F32)
            inner = (scores * decay_ref[h]).astype(BF16)
            state = state_ref[bi, h]
            out = (jnp.dot(inner, vb, preferred_element_type=F32)
                   + jnp.dot(qb, state.astype(BF16), preferred_element_type=F32) * xi_ref[h])
            vz = (vb.astype(F32) * zeta_ref[h]).astype(BF16)
            state_ref[bi, h] = state * gc_ref[h] + lax.dot_general(
                kb, vz, (((0,), (0,)), ((), ())), preferred_element_type=F32)

            mu = jnp.mean(out, axis=-1, keepdims=True)
            cen = out - mu
            var = jnp.mean(cen * cen, axis=-1, keepdims=True)
            normed = cen * lax.rsqrt(var + GN_EPS) * gain_ref[h]
            gate = g_ref[bi, :, cols].astype(F32)
            o_ref[bi, :, cols] = (normed * (gate * jax.nn.sigmoid(gate))).astype(o_ref.dtype)


def _retention_constants():
    d, c = RET_HEAD_DIM, RET_CHUNK
    log_g = np.log1p(-np.exp2(-5.0 - np.arange(RET_HEADS, dtype=np.float64)))
    idx = np.arange(c, dtype=np.float64)
    rel = idx[:, None] - idx[None, :]
    decay = np.where(rel >= 0, np.exp(np.maximum(rel, 0.0)[None] * log_g[:, None, None]), 0.0)
    xi = np.exp((idx + 1.0)[None] * log_g[:, None])
    zeta = np.exp((c - 1.0 - idx)[None] * log_g[:, None])
    g_chunk = np.exp(c * log_g)
    bcast = lambda t: np.ascontiguousarray(np.broadcast_to(t[..., None], t.shape + (d,)), dtype=np.float32)
    return decay.astype(np.float32), bcast(xi), bcast(zeta), bcast(g_chunk[:, None])


def _retention(z3d, ret_gn_gain):
    b, s, _ = z3d.shape
    c, d = RET_CHUNK, RET_HEAD_DIM
    base = 4 * ATTN_WIDTH // RET_WIDTH
    decay, xi, zeta, gc = _retention_constants()
    gain = ret_gn_gain.astype(F32).reshape(RET_HEADS, 1, d)
    zspec = lambda off: pl.BlockSpec((b, c, RET_WIDTH), lambda ci: (0, ci, base + off))
    whole = lambda *shape: pl.BlockSpec(shape, lambda ci: (0,) * len(shape))
    return pl.pallas_call(
        _retention_kernel,
        grid=(s // c,),
        in_specs=[zspec(0), zspec(1), zspec(2), zspec(3),
                  whole(RET_HEADS, c, c), whole(RET_HEADS, c, d), whole(RET_HEADS, c, d),
                  whole(RET_HEADS, 1, d), whole(RET_HEADS, 1, d)],
        out_specs=pl.BlockSpec((b, c, RET_WIDTH), lambda ci: (0, ci, 0)),
        out_shape=jax.ShapeDtypeStruct((b, s, RET_WIDTH), BF16),
        scratch_shapes=[pltpu.VMEM((b, RET_HEADS, d, d), F32)],
        compiler_params=pltpu.CompilerParams(dimension_semantics=("arbitrary",),
                                             vmem_limit_bytes=VMEM_LIMIT),
        name="retention",
    )(z3d, z3d, z3d, z3d, decay, xi, zeta, gc, gain)


def _output_kernel(x_hbm, a_ref, r_ref, p_ref, wo32_ref, wg32_ref, wp32_ref, gain_ref, bias_ref, o_ref,
                   wo_ref, wg_ref, wp_ref, x_ring, x_sem, *, sub):
    i = pl.program_id(0)
    n_steps = pl.num_programs(0)
    tm = o_ref.shape[0]
    n_slots = x_ring.shape[0]
    slot = i % n_slots

    def x_copy(step):
        return pltpu.make_async_copy(x_hbm.at[pl.ds(pl.multiple_of(step * tm, tm), tm), :],
                                     x_ring.at[step % n_slots], x_sem.at[step % n_slots])

    @pl.when(i == 0)
    def _first_step():
        x_copy(i).start()
        x_copy(i + 1).start()
        wo_ref[...] = wo32_ref[...].astype(BF16)
        wg_ref[...] = wg32_ref[...].astype(BF16)
        wp_ref[...] = wp32_ref[...].astype(BF16)

    @pl.when(i + 2 < n_steps)
    def _fetch_two_steps_ahead():
        x_copy(i + 2).start()

    x_copy(i).wait()

    def matmuls(rows):
        mix = (jnp.dot(a_ref[rows, :], wo_ref[0:ATTN_WIDTH, :], preferred_element_type=F32)
               + jnp.dot(r_ref[rows, :], wo_ref[ATTN_WIDTH:, :], preferred_element_type=F32))
        u = DEEPNORM_ALPHA * x_ring[slot, rows, :] + mix
        ple = jnp.dot(p_ref[rows, :].astype(BF16), wp_ref[...], preferred_element_type=F32)
        gate_in = jnp.dot(u.astype(BF16), wg_ref[...], preferred_element_type=F32)
        return u, ple, gate_in

    def tail(rows, u, ple, gate_in):
        u = u + jax.nn.sigmoid(gate_in) * ple
        mu = jnp.mean(u, axis=-1, keepdims=True)
        cen = u - mu
        var = jnp.mean(cen * cen, axis=-1, keepdims=True)
        o_ref[rows, :] = cen * lax.rsqrt(var + LN_EPS) * gain_ref[...] + bias_ref[...]

    n_sub = o_ref.shape[0] // sub
    rows = [slice(r * sub, (r + 1) * sub) for r in range(n_sub)]
    pending = matmuls(rows[0])
    for r in range(n_sub):
        following = matmuls(rows[r + 1]) if r + 1 < n_sub else None
        tail(rows[r], *pending)
        pending = following


def _output_stage(x2d, attn2d, ret2d, p2d, w_out, w_gate, w_proj, ln_gain, ln_bias, *, tm=1024, sub=256):
    m, dm = x2d.shape
    assert m // tm >= 2
    rows = lambda width: pl.BlockSpec((tm, width), lambda i: (i, 0))
    return pl.pallas_call(
        functools.partial(_output_kernel, sub=sub),
        grid=(m // tm,),
        in_specs=[pl.BlockSpec(memory_space=pl.ANY), rows(ATTN_WIDTH), rows(RET_WIDTH), rows(D_PLE),
                  _resident(w_out.shape), _resident(w_gate.shape), _resident(w_proj.shape),
                  _resident((1, dm)), _resident((1, dm))],
        out_specs=rows(dm),
        out_shape=jax.ShapeDtypeStruct((m, dm), F32),
        scratch_shapes=[pltpu.VMEM(w_out.shape, BF16), pltpu.VMEM(w_gate.shape, BF16),
                        pltpu.VMEM(w_proj.shape, BF16),
                        pltpu.VMEM((3, tm, dm), F32), pltpu.SemaphoreType.DMA((3,))],
        compiler_params=pltpu.CompilerParams(dimension_semantics=("arbitrary",),
                                             vmem_limit_bytes=VMEM_LIMIT),
        name="output_stage",
    )(x2d, attn2d, ret2d, p2d, w_out, w_gate, w_proj,
      ln_gain.astype(F32).reshape(1, dm), ln_bias.astype(F32).reshape(1, dm))


def kernel(x, p, w_in, w_out, ret_gn_gain, w_ple_gate, w_ple_proj, ln_gain, ln_bias):
    b, s, dm = x.shape
    h = x
    for i in range(DEPTH):
        h2d = h.reshape(b * s, dm)
        z = _input_projection(h2d, w_in[i].astype(F32), s)
        z3d = z.reshape(b, s, IN_WIDTH)
        attn = _moba_attention(z3d)
        ret = _retention(z3d, ret_gn_gain[i])
        out = _output_stage(h2d, attn.reshape(b * s, ATTN_WIDTH), ret.reshape(b * s, RET_WIDTH),
                            p[i].reshape(b * s, D_PLE), w_out[i].astype(F32), w_ple_gate[i].astype(F32),
                            w_ple_proj[i].astype(F32), ln_gain[i], ln_bias[i])
        h = out.reshape(b, s, dm)
    return h
```

```python
import functools

import jax
import jax.numpy as jnp
import numpy as np
from jax import lax
from jax.experimental import pallas as pl
from jax.experimental.pallas import tpu as pltpu

F32 = jnp.float32
BF16 = jnp.bfloat16

D_MODEL = 1024
D_PLE = 256
ATTN_HEADS = 8
ATTN_HEAD_DIM = 64
ATTN_WIDTH = ATTN_HEADS * ATTN_HEAD_DIM
RET_HEADS = 4
RET_HEAD_DIM = 128
RET_WIDTH = RET_HEADS * RET_HEAD_DIM
IN_WIDTH = 4 * ATTN_WIDTH + 4 * RET_WIDTH
MOBA_BLOCK = 256
MOBA_TOPK = 3
ROPE_BASE = 10000.0
DEPTH = 1
DEEPNORM_ALPHA = (2.0 * DEPTH) ** 0.25
LN_EPS = 1e-5
GN_EPS = 1e-6

LANES = 128
HEADS_PER_STEP = LANES // ATTN_HEAD_DIM
MASKED = -1e30
LOG2_E = 1.4426950408889634
ONES_ROWS = 16
CHUNK_BLOCKS = 4
RET_CHUNK = 256
RET_CHUNKS_PER_STEP = 2
VMEM_LIMIT = 48 * 1024 * 1024


def _proj_kernel(x_ref, w32_ref, cos_ref, sin_ref, o_ref, w_ref, *, n_chunk):
    @pl.when(pl.program_id(0) == 0)
    def _cast_weights_once():
        for lo in range(0, w_ref.shape[1], n_chunk):
            w_ref[:, lo:lo + n_chunk] = w32_ref[:, lo:lo + n_chunk].astype(BF16)

    xb = x_ref[...].astype(BF16)
    d = RET_HEAD_DIM
    rot_lo = 4 * ATTN_WIDTH
    rot_mid = rot_lo + RET_WIDTH
    rot_hi = rot_mid + RET_WIDTH
    assert rot_lo % n_chunk == 0 and rot_hi % n_chunk == 0 and n_chunk % d == 0
    for lo in range(0, o_ref.shape[1], n_chunk):
        acc = jnp.dot(xb, w_ref[:, lo:lo + n_chunk], preferred_element_type=F32)
        if rot_lo <= lo < rot_hi:
            cos = cos_ref[...]
            sin = sin_ref[...]
            for col in range(lo, lo + n_chunk, d):
                head = acc[:, col - lo:col - lo + d]
                rotated = head * cos + pltpu.roll(head, d // 2, 1) * sin
                if col >= rot_mid:
                    rotated = rotated * (d ** -0.5)
                o_ref[:, col:col + d] = rotated.astype(o_ref.dtype)
        else:
            o_ref[:, lo:lo + n_chunk] = acc.astype(o_ref.dtype)


def _rotary_tables(s):
    half = RET_HEAD_DIM // 2
    freqs = ROPE_BASE ** (-np.arange(half, dtype=np.float64) / half)
    ang = np.arange(s, dtype=np.float64)[:, None] * freqs[None, :]
    cos, sin = np.cos(ang), np.sin(ang)
    return (np.concatenate([cos, cos], axis=-1).astype(np.float32),
            np.concatenate([-sin, sin], axis=-1).astype(np.float32))


def _resident(shape):
    return pl.BlockSpec(shape, lambda *_: (0,) * len(shape), pipeline_mode=pl.Buffered(1))


def _input_projection(x2d, w_in, seq, *, tm=512, n_chunk=1024):
    m, k = x2d.shape
    n = w_in.shape[1]
    cos, sin = _rotary_tables(seq)
    pos = pl.BlockSpec((tm, RET_HEAD_DIM), lambda i: (i % (seq // tm), 0))
    return pl.pallas_call(
        functools.partial(_proj_kernel, n_chunk=n_chunk),
        grid=(m // tm,),
        in_specs=[pl.BlockSpec((tm, k), lambda i: (i, 0)), _resident((k, n)), pos, pos],
        out_specs=pl.BlockSpec((tm, n), lambda i: (i, 0)),
        out_shape=jax.ShapeDtypeStruct((m, n), BF16),
        scratch_shapes=[pltpu.VMEM((k, n), BF16)],
        compiler_params=pltpu.CompilerParams(dimension_semantics=("arbitrary",),
                                             vmem_limit_bytes=VMEM_LIMIT),
        name="input_projection",
    )(x2d, w_in, cos, sin)


def _moba_kernel(q_ref, k_ref, v_ref, g_ref, onehot_ref, o_ref,
                 vtc_ref, qtb_ref, s0_ref, s1_ref, mrun_ref, mnow_ref, alpha_ref, acc_ref, *, nb):
    blk = MOBA_BLOCK
    hd = ATTN_HEAD_DIM
    nh = HEADS_PER_STEP
    width = nh * blk
    vrows = LANES + ONES_ROWS
    seq = nb * blk
    chunk = CHUNK_BLOCKS * blk
    n_chunks = nb // CHUNK_BLOCKS
    s_refs = (s0_ref, s1_ref)

    @pl.when((pl.program_id(0) == 0) & (pl.program_id(1) == 0))
    def _fill_constant_regions():
        vtc_ref[:, LANES:vrows, :] = jnp.ones((n_chunks, ONES_ROWS, chunk), BF16)
        qtb_ref[...] = jnp.zeros(qtb_ref.shape, BF16)

    def _prepare_head_pair():
        for n in range(nb):
            c, j = divmod(n, CHUNK_BLOCKS)
            vtc_ref[c, 0:LANES, j * blk:(j + 1) * blk] = (
                v_ref[0, n * blk:(n + 1) * blk, :].astype(F32).T.astype(BF16))
        key_blk = lax.broadcasted_iota(jnp.int32, (nb, seq), 1) // blk
        row_blk = lax.broadcasted_iota(jnp.int32, (nb, seq), 0)
        averager = jnp.where(key_blk == row_blk, 1.0 / blk, 0.0).astype(BF16)
        kmean = jnp.dot(averager, k_ref[0], preferred_element_type=F32)
        hi = kmean.astype(BF16)
        lo = (kmean - hi.astype(F32)).astype(BF16)
        lane = lax.broadcasted_iota(jnp.int32, (nb, LANES), 1)
        parts = []
        for h in range(nh):
            in_head = (lane >= h * hd) & (lane < (h + 1) * hd)
            parts += [jnp.where(in_head, hi, jnp.zeros_like(hi)), jnp.where(in_head, lo, jnp.zeros_like(lo))]
        kstack = jnp.concatenate(parts, axis=0)
        qts = []
        for n in range(nb):
            qt = (q_ref[0, n * blk:(n + 1) * blk, :].astype(F32) * (hd ** -0.5 * LOG2_E)).T.astype(BF16)
            for h in range(nh):
                qtb_ref[n, h * hd:(h + 1) * hd, h * blk:(h + 1) * blk] = qt[h * hd:(h + 1) * hd]
            qts.append(qt)
        gates = jnp.dot(kstack, jnp.concatenate(qts, axis=1), preferred_element_type=F32)
        past = row_blk < key_blk
        row_f = row_blk.astype(F32)
        own_f = jnp.where(row_blk == key_blk, 1.0, 0.0)
        for h in range(nh):
            g = gates[(2 * h) * nb:(2 * h + 1) * nb] + gates[(2 * h + 1) * nb:(2 * h + 2) * nb]
            g = jnp.where(past, g, -jnp.inf)
            picked = jnp.zeros((nb, seq), F32)
            for _ in range(MOBA_TOPK):
                top = jnp.max(g, axis=0, keepdims=True)
                first = jnp.min(jnp.where(g == top, row_f, float(nb)), axis=0, keepdims=True)
                pick = row_f == first
                picked = jnp.where(pick, 1.0, picked)
                g = jnp.where(pick, -jnp.inf, g)
            bias = jnp.where(jnp.where(past, picked, own_f) > 0.5, 0.0, MASKED).astype(BF16)
            for n in range(nb):
                qtb_ref[n, LANES:LANES + nb, h * blk:(h + 1) * blk] = bias[:, n * blk:(n + 1) * blk]

    _prepare_head_pair()

    key_pos = lax.broadcasted_iota(jnp.int32, (blk, width), 0)
    q_pos = lax.broadcasted_iota(jnp.int32, (blk, width), 1) & (blk - 1)
    causal = key_pos <= q_pos

    def rows_at(start, size):
        if isinstance(start, int):
            return pl.ds(start, size)
        return pl.ds(pl.multiple_of(start, blk), size)

    def scores(c, qi, slot, cnt, own_j=None):
        key_rows = rows_at(c * chunk, cnt * blk)
        keys_ext = jnp.concatenate([k_ref[0, key_rows, :], onehot_ref[key_rows, :]], axis=1)
        s = jnp.dot(keys_ext, qtb_ref[qi], preferred_element_type=F32)
        m8 = None
        for j in range(cnt):
            sj = s[j * blk:(j + 1) * blk]
            if j == own_j:
                sj = jnp.where(causal, sj, MASKED)
            s_refs[slot][j * blk:(j + 1) * blk, :] = sj
            mj = jnp.max(sj.reshape(blk // 8, 8, width), axis=0)
            m8 = mj if m8 is None else jnp.maximum(m8, mj)
        m_new = jnp.broadcast_to(jnp.max(m8, axis=0, keepdims=True), (8, width))
        if c > 0:
            m_old = mrun_ref[qi]
            m_new = jnp.maximum(m_old, m_new)
            alpha_ref[slot] = jnp.exp2(m_old - m_new)
        mrun_ref[qi] = m_new
        mnow_ref[slot] = m_new

    def values(c, qi, slot, cnt):
        p = jnp.exp2(s_refs[slot][0:cnt * blk, :] - mnow_ref[slot, 0:1, :]).astype(BF16)
        pv = jnp.dot(vtc_ref[c, :, 0:cnt * blk], p, preferred_element_type=F32)
        if c > 0:
            pv = acc_ref[qi] * alpha_ref[slot, 0:1, :] + pv
        acc_ref[qi] = pv

    def finish(qi):
        outs = []
        for h in range(nh):
            cols = slice(h * blk, (h + 1) * blk)
            outs.append(acc_ref[qi, h * hd:(h + 1) * hd, cols] / acc_ref[qi, LANES:LANES + 1, cols])
        attn = jnp.concatenate(outs, axis=0).T
        gate = g_ref[0, rows_at(qi * blk, blk), :].astype(F32)
        o_ref[0, rows_at(qi * blk, blk), :] = (attn * (gate * jax.nn.sigmoid(gate))).astype(o_ref.dtype)

    assert CHUNK_BLOCKS % 2 == 0 and nb % CHUNK_BLOCKS == 0

    def chunk_section(c):
        first = c * CHUNK_BLOCKS
        later = first + CHUNK_BLOCKS
        scores(c, first, 0, 1, own_j=0)
        for j in range(CHUNK_BLOCKS):
            if j + 1 < CHUNK_BLOCKS:
                scores(c, first + j + 1, (j + 1) % 2, j + 2, own_j=j + 1)
            elif later < nb:
                scores(c, later, 0, CHUNK_BLOCKS)
            values(c, first + j, j % 2, j + 1)
            finish(first + j)
        for qi in range(later, nb):
            if qi + 1 < nb:
                scores(c, qi + 1, (qi + 1 - later) % 2, CHUNK_BLOCKS)
            values(c, qi, (qi - later) % 2, CHUNK_BLOCKS)

    chunk_section(0)

    @pl.when(pl.program_id(1) >= 0)
    def _remaining_chunks():
        for c in range(1, n_chunks):
            chunk_section(c)


def _moba_attention(z3d):
    b, s, _ = z3d.shape
    nb = s // MOBA_BLOCK
    steps = ATTN_WIDTH // LANES
    width = HEADS_PER_STEP * MOBA_BLOCK
    vrows = LANES + ONES_ROWS
    kernel = functools.partial(_moba_kernel, nb=nb)
    full = lambda off: pl.BlockSpec((1, s, LANES), lambda bi, hp: (bi, 0, off * steps + hp))
    onehot = jnp.asarray(np.arange(s)[:, None] // MOBA_BLOCK == np.arange(LANES)[None, :], dtype=BF16)
    return pl.pallas_call(
        kernel,
        grid=(b, steps),
        in_specs=[full(0), full(1), full(2), full(3), _resident((s, LANES))],
        out_specs=pl.BlockSpec((1, s, LANES), lambda bi, hp: (bi, 0, hp)),
        out_shape=jax.ShapeDtypeStruct((b, s, ATTN_WIDTH), BF16),
        scratch_shapes=[pltpu.VMEM((nb // CHUNK_BLOCKS, vrows, CHUNK_BLOCKS * MOBA_BLOCK), BF16),
                        pltpu.VMEM((nb, 2 * LANES, width), BF16),
                        pltpu.VMEM((CHUNK_BLOCKS * MOBA_BLOCK, width), F32),
                        pltpu.VMEM((CHUNK_BLOCKS * MOBA_BLOCK, width), F32),
                        pltpu.VMEM((nb, 8, width), F32),
                        pltpu.VMEM((2, 8, width), F32),
                        pltpu.VMEM((2, 8, width), F32),
                        pltpu.VMEM((nb, vrows, width), F32)],
        compiler_params=pltpu.CompilerParams(dimension_semantics=("arbitrary",) * 2,
                                             vmem_limit_bytes=VMEM_LIMIT),
        name="moba_attention",
    )(z3d, z3d, z3d, z3d, onehot)


def _retention_kernel(q_ref, k_ref, v_ref, g_ref, decay_ref, xi_ref, zeta_ref, gc_ref, gain_ref,
                      o_ref, state_ref):
    c = pl.program_id(0)
    d = RET_HEAD_DIM

    @pl.when(c == 0)
    def _reset_state():
        state_ref[...] = jnp.zeros_like(state_ref)

    for bi in range(q_ref.shape[0]):
        for h in range(RET_HEADS):
            cols = slice(h * d, (h + 1) * d)
            state = state_ref[bi, h]
            for ci in range(q_ref.shape[1] // RET_CHUNK):
                rows = slice(ci * RET_CHUNK, (ci + 1) * RET_CHUNK)
                qb = q_ref[bi, rows, cols]
                kb = k_ref[bi, rows, cols]
                vb = v_ref[bi, rows, cols]
                scores = lax.dot_general(qb, kb, (((1,), (1,)), ((), ())), preferred_element_type=F32)
                inner = (scores * decay_ref[h]).astype(BF16)
                out = (jnp.dot(inner, vb, preferred_element_type=F32)
                       + jnp.dot(qb, state.astype(BF16), preferred_element_type=F32) * xi_ref[h])
                vz = (vb.astype(F32) * zeta_ref[h]).astype(BF16)
                state = state * gc_ref[h] + lax.dot_general(
                    kb, vz, (((0,), (0,)), ((), ())), preferred_element_type=F32)

                mu = jnp.mean(out, axis=-1, keepdims=True)
                cen = out - mu
                var = jnp.mean(cen * cen, axis=-1, keepdims=True)
                normed = cen * lax.rsqrt(var + GN_EPS) * gain_ref[h]
                gate = g_ref[bi, rows, cols].astype(F32)
                o_ref[bi, rows, cols] = (normed * (gate * jax.nn.sigmoid(gate))).astype(o_ref.dtype)
            state_ref[bi, h] = state


def _retention_constants():
    d, c = RET_HEAD_DIM, RET_CHUNK
    log_g = np.log1p(-np.exp2(-5.0 - np.arange(RET_HEADS, dtype=np.float64)))
    idx = np.arange(c, dtype=np.float64)
    rel = idx[:, None] - idx[None, :]
    decay = np.where(rel >= 0, np.exp(np.maximum(rel, 0.0)[None] * log_g[:, None, None]), 0.0)
    xi = np.exp((idx + 1.0)[None] * log_g[:, None])
    zeta = np.exp((c - 1.0 - idx)[None] * log_g[:, None])
    g_chunk = np.exp(c * log_g)
    bcast = lambda t: np.ascontiguousarray(np.broadcast_to(t[..., None], t.shape + (d,)), dtype=np.float32)
    return decay.astype(np.float32), bcast(xi), bcast(zeta), bcast(g_chunk[:, None])


def _retention(z3d, ret_gn_gain):
    b, s, _ = z3d.shape
    c, d = RET_CHUNK, RET_HEAD_DIM
    base = 4 * ATTN_WIDTH // RET_WIDTH
    decay, xi, zeta, gc = _retention_constants()
    gain = ret_gn_gain.astype(F32).reshape(RET_HEADS, 1, d)
    rows = RET_CHUNKS_PER_STEP * c
    zspec = lambda off: pl.BlockSpec((b, rows, RET_WIDTH), lambda ci: (0, ci, base + off))
    whole = lambda *shape: pl.BlockSpec(shape, lambda ci: (0,) * len(shape))
    return pl.pallas_call(
        _retention_kernel,
        grid=(s // rows,),
        in_specs=[zspec(0), zspec(1), zspec(2), zspec(3),
                  whole(RET_HEADS, c, c), whole(RET_HEADS, c, d), whole(RET_HEADS, c, d),
                  whole(RET_HEADS, 1, d), whole(RET_HEADS, 1, d)],
        out_specs=pl.BlockSpec((b, rows, RET_WIDTH), lambda ci: (0, ci, 0)),
        out_shape=jax.ShapeDtypeStruct((b, s, RET_WIDTH), BF16),
        scratch_shapes=[pltpu.VMEM((b, RET_HEADS, d, d), F32)],
        compiler_params=pltpu.CompilerParams(dimension_semantics=("arbitrary",),
                                             vmem_limit_bytes=VMEM_LIMIT),
        name="retention",
    )(z3d, z3d, z3d, z3d, decay, xi, zeta, gc, gain)


def _output_kernel(x_ref, a_ref, r_ref, p_ref, wo32_ref, wg32_ref, wp32_ref, gain_ref, bias_ref, o_ref,
                   wo_ref, wg_ref, wp_ref, *, sub):
    @pl.when(pl.program_id(0) == 0)
    def _cast_weights_once():
        wo_ref[...] = wo32_ref[...].astype(BF16)
        wg_ref[...] = wg32_ref[...].astype(BF16)
        wp_ref[...] = wp32_ref[...].astype(BF16)

    def matmuls(rows):
        mix = (jnp.dot(a_ref[rows, :], wo_ref[0:ATTN_WIDTH, :], preferred_element_type=F32)
               + jnp.dot(r_ref[rows, :], wo_ref[ATTN_WIDTH:, :], preferred_element_type=F32))
        u = DEEPNORM_ALPHA * x_ref[rows, :] + mix
        ple = jnp.dot(p_ref[rows, :].astype(BF16), wp_ref[...], preferred_element_type=F32)
        gate_in = jnp.dot(u.astype(BF16), wg_ref[...], preferred_element_type=F32)
        return u, ple, gate_in

    def tail(rows, u, ple, gate_in):
        u = u + jax.nn.sigmoid(gate_in) * ple
        mu = jnp.mean(u, axis=-1, keepdims=True)
        cen = u - mu
        var = jnp.mean(cen * cen, axis=-1, keepdims=True)
        o_ref[rows, :] = cen * lax.rsqrt(var + LN_EPS) * gain_ref[...] + bias_ref[...]

    n_sub = o_ref.shape[0] // sub
    rows = [slice(r * sub, (r + 1) * sub) for r in range(n_sub)]
    pending = matmuls(rows[0])
    for r in range(n_sub):
        following = matmuls(rows[r + 1]) if r + 1 < n_sub else None
        tail(rows[r], *pending)
        pending = following


def _output_stage(x2d, attn2d, ret2d, p2d, w_out, w_gate, w_proj, ln_gain, ln_bias, *, tm=1024, sub=256):
    m, dm = x2d.shape
    rows = lambda width: pl.BlockSpec((tm, width), lambda i: (i, 0))
    return pl.pallas_call(
        functools.partial(_output_kernel, sub=sub),
        grid=(m // tm,),
        in_specs=[rows(dm), rows(ATTN_WIDTH), rows(RET_WIDTH), rows(D_PLE),
                  _resident(w_out.shape), _resident(w_gate.shape), _resident(w_proj.shape),
                  _resident((1, dm)), _resident((1, dm))],
        out_specs=rows(dm),
        out_shape=jax.ShapeDtypeStruct((m, dm), F32),
        scratch_shapes=[pltpu.VMEM(w_out.shape, BF16), pltpu.VMEM(w_gate.shape, BF16),
                        pltpu.VMEM(w_proj.shape, BF16)],
        compiler_params=pltpu.CompilerParams(dimension_semantics=("arbitrary",),
                                             vmem_limit_bytes=VMEM_LIMIT),
        name="output_stage",
    )(x2d, attn2d, ret2d, p2d, w_out, w_gate, w_proj,
      ln_gain.astype(F32).reshape(1, dm), ln_bias.astype(F32).reshape(1, dm))


def kernel(x, p, w_in, w_out, ret_gn_gain, w_ple_gate, w_ple_proj, ln_gain, ln_bias):
    b, s, dm = x.shape
    h = x
    for i in range(DEPTH):
        h2d = h.reshape(b * s, dm)
        z = _input_projection(h2d, w_in[i].astype(F32), s)
        z3d = z.reshape(b, s, IN_WIDTH)
        attn = _moba_attention(z3d)
        ret = _retention(z3d, ret_gn_gain[i])
        out = _output_stage(h2d, attn.reshape(b * s, ATTN_WIDTH), ret.reshape(b * s, RET_WIDTH),
                            p[i].reshape(b * s, D_PLE), w_out[i].astype(F32), w_ple_gate[i].astype(F32),
                            w_ple_proj[i].astype(F32), ln_gain[i], ln_bias[i])
        h = out.reshape(b, s, dm)
    return h
```
